```python
import math
import jax
import jax.numpy as jnp
from jax import lax
import numpy as np

D_MODEL = 1024
BATCH = 1
SEQ = 16384
DEPTH = 2

GRID_W = 64
CTX_LEN = 256
EPS = 1e-6
N_MOD = 6

SSD_HEADS = 16
SSD_HEAD_DIM = 64
SSD_INNER = SSD_HEADS * SSD_HEAD_DIM
SSD_GROUPS = 2
SSD_STATE = 128
SSD_GN = SSD_GROUPS * SSD_STATE
SSD_CONV = 3
SSD_CHUNK = 128

DIFF_HEADS = 8
DIFF_HEAD_DIM = 64
DIFF_WIDTH = DIFF_HEADS * 2 * DIFF_HEAD_DIM
ROPE_BASE = 10000.0
Q_BLOCK = 128

CONF_WIDTH = D_MODEL
CONF_KERNEL = 31

N_BRANCHES = 3

N_EXPERTS = 16
EXPERT_FF = 2 * D_MODEL
CAPACITY_FACTOR = 2

COL_SIZES = (SSD_INNER, SSD_GN, 2 * SSD_HEADS, DIFF_WIDTH, DIFF_WIDTH, SSD_GN, DIFF_WIDTH, SSD_INNER, 2 * CONF_WIDTH, N_BRANCHES * D_MODEL)
N_COLS = sum(COL_SIZES)
N_STATE_COLS = sum(COL_SIZES[:5])

kernel_name = 'hybrid_ssd_diffattn_conformer_ecmoe_dit'


def split_last(a, sizes):
    idx = np.cumsum(sizes)[:-1].tolist()
    return jnp.split(a, idx, axis=-1)


def rms_norm(x, w, eps=EPS):
    xf = x.astype(jnp.float32)
    y = xf * lax.rsqrt(jnp.mean(xf * xf, axis=-1, keepdims=True) + eps)
    return (y * w.astype(jnp.float32)).astype(x.dtype)


def layer_norm(x, w, b, eps=EPS):
    xf = x.astype(jnp.float32)
    mu = jnp.mean(xf, axis=-1, keepdims=True)
    xc = xf - mu
    y = xc * lax.rsqrt(jnp.mean(xc * xc, axis=-1, keepdims=True) + eps)
    return (y * w.astype(jnp.float32) + b.astype(jnp.float32)).astype(x.dtype)


def modulate(x, w, shift, scale):
    return rms_norm(x, w) * (1 + scale) + shift


def flip_seq(a):
    return jnp.flip(a, axis=1)


def dwconv(u, w, b):
    k = w.shape[0]
    out = lax.conv_general_dilated(u, w.astype(u.dtype)[:, None, :], window_strides=(1,), padding=[(k // 2, k // 2)], dimension_numbers=('NWC', 'WIO', 'NWC'), feature_group_count=u.shape[-1])
    return out + b.astype(u.dtype)


def axial_rope_tables(n_rows):
    row = jnp.repeat(jnp.arange(n_rows, dtype=jnp.float32), GRID_W)
    col = jnp.tile(jnp.arange(GRID_W, dtype=jnp.float32), n_rows)
    half = DIFF_HEAD_DIM // 2
    freqs = ROPE_BASE ** (-jnp.arange(0, half, 2, dtype=jnp.float32) / half)
    ang = jnp.concatenate([row[:, None] * freqs, col[:, None] * freqs], axis=-1)
    return jnp.cos(ang), jnp.sin(ang)


def apply_axial_rope(t, cos, sin):
    n = t.shape[1]
    q4 = DIFF_HEAD_DIM // 4
    tr = t.reshape(t.shape[:-1] + (2, 2, q4))
    cs = cos.reshape(n, 1, 1, 2, q4).astype(t.dtype)
    sn = sin.reshape(n, 1, 1, 2, q4).astype(t.dtype)
    t1, t2 = tr[..., 0, :], tr[..., 1, :]
    out = jnp.stack([t1 * cs - t2 * sn, t2 * cs + t1 * sn], axis=-2)
    return out.reshape(t.shape)


def expand_groups(m):
    b, n = m.shape[0], m.shape[1]
    return jnp.repeat(m.reshape(b, n, SSD_GROUPS, SSD_STATE), SSD_HEADS // SSD_GROUPS, axis=2)


def ssd_steps(pdt, dt_bias, a_log):
    b, n = pdt.shape[0], pdt.shape[1]
    dt = jax.nn.softplus(pdt.reshape(b, n, 2, SSD_HEADS).astype(jnp.float32) + dt_bias.astype(jnp.float32))
    return dt, dt * (-jnp.exp(a_log.astype(jnp.float32)))


def ssd_scan(X, A, Bm, Cm, h0):
    b, l, h, p = X.shape
    n = Bm.shape[-1]
    nc, L = l // SSD_CHUNK, SSD_CHUNK
    Xc = X.reshape(b, nc, L, h, p)
    Bc = Bm.reshape(b, nc, L, h, n)
    Cc = Cm.reshape(b, nc, L, h, n)
    Acum = jnp.cumsum(A.astype(jnp.float32).reshape(b, nc, L, h), axis=2)
    seg = Acum[:, :, :, None, :] - Acum[:, :, None, :, :]
    lower = jnp.tril(jnp.ones((L, L), dtype=bool))[None, None, :, :, None]
    decay = jnp.exp(jnp.where(lower, seg, -jnp.inf))
    scores = jnp.einsum('bclhn,bcshn->bclsh', Cc, Bc) * decay
    y_diag = jnp.einsum('bclsh,bcshp->bclhp', scores, Xc)
    to_end = jnp.exp(Acum[:, :, -1:, :] - Acum)
    states = jnp.einsum('bclhn,bclh,bclhp->bchpn', Bc, to_end, Xc)
    chunk_decay = jnp.exp(Acum[:, :, -1, :])

    def step(hc, inp):
        st, dec = inp
        return hc * dec[:, :, None, None] + st, hc

    final, h_in = lax.scan(step, h0.astype(jnp.float32), (jnp.moveaxis(states, 1, 0), jnp.moveaxis(chunk_decay, 1, 0)))
    h_in = jnp.moveaxis(h_in, 0, 1)
    y_off = jnp.einsum('bclhn,bchpn,bclh->bclhp', Cc, h_in, jnp.exp(Acum))
    return (y_diag + y_off).reshape(b, l, h, p), final


def ssd_final_state(X, A, Bm):
    Acum = jnp.cumsum(A.astype(jnp.float32), axis=1)
    w = jnp.exp(Acum[:, -1:, :] - Acum)
    return jnp.einsum('blhn,blh,blhp->bhpn', Bm, w, X)


def ssd_bidirectional(xs, Bm, Cm, pdt, dt_bias, a_log, d_skip, h0_f, h0_b):
    b, n = xs.shape[0], xs.shape[1]
    xs = xs.reshape(b, n, SSD_HEADS, SSD_HEAD_DIM)
    Bh, Ch = expand_groups(Bm), expand_groups(Cm)
    dt, a = ssd_steps(pdt, dt_bias, a_log)
    y_f, s_f = ssd_scan(xs * dt[:, :, 0, :, None], a[:, :, 0], Bh, Ch, h0_f)
    y_b, s_b = ssd_scan(flip_seq(xs * dt[:, :, 1, :, None]), flip_seq(a[:, :, 1]), flip_seq(Bh), flip_seq(Ch), h0_b)
    y = y_f + flip_seq(y_b) + xs * d_skip[:, None].astype(xs.dtype)
    return y.reshape(b, n, SSD_INNER), s_f, s_b


def gated_norm(y, z, w):
    b, n = y.shape[0], y.shape[1]
    g = SSD_INNER // SSD_GROUPS
    yz = (y * jax.nn.silu(z)).reshape(b, n, SSD_GROUPS, g)
    return rms_norm(yz, w.reshape(SSD_GROUPS, g)).reshape(b, n, SSD_INNER)


def diff_attention(q, k, v, lam):
    b, n = q.shape[0], q.shape[1]
    nb = n // Q_BLOCK
    qb = jnp.moveaxis(q.reshape((b, nb, Q_BLOCK) + q.shape[2:]), 1, 0)
    scale = DIFF_HEAD_DIM ** -0.5

    def block(qi):
        s = jnp.einsum('bqhmd,bkhmd->bhmqk', qi, k, preferred_element_type=jnp.float32) * scale
        a = jax.nn.softmax(s, axis=-1)
        w = a[:, :, 0] - lam * a[:, :, 1]
        return jnp.einsum('bhqk,bkhe->bqhe', w.astype(v.dtype), v)

    o = lax.map(block, qb)
    return jnp.moveaxis(o, 0, 1).reshape((b, n) + o.shape[3:])


def conformer_conv(pglu, dw_w, dw_b, ln_w, ln_b, w_out):
    a, g = jnp.split(pglu, 2, axis=-1)
    u = dwconv(a * jax.nn.sigmoid(g), dw_w, dw_b)
    return jax.nn.silu(layer_norm(u, ln_w, ln_b)) @ w_out


def gated_merge(pgate, y_a, y_b, y_c, w_o):
    g = jax.nn.sigmoid(pgate.astype(jnp.float32)).reshape(pgate.shape[:-1] + (N_BRANCHES, D_MODEL))
    m = g[..., 0, :] * y_a + g[..., 1, :] * y_b + g[..., 2, :] * y_c
    return m.astype(w_o.dtype) @ w_o


def token_mixer(h, w_in, mp, h0_f, h0_b, kv_prefix, rope, lam, lam_init):
    (conv_w, conv_b, dt_bias, a_log, d_skip, ssd_norm_w, ssd_out, subln_w, diff_out, dw_w, dw_b, ln_w, ln_b, conf_out, w_o) = mp
    b, n = h.shape[0], h.shape[1]
    px, pB, pdt, pk, pv, pC, pq, pz, pglu, pgate = split_last(h @ w_in, COL_SIZES)
    xbc = jax.nn.silu(dwconv(jnp.concatenate([px, pB, pC], axis=-1), conv_w, conv_b))
    xs, Bm, Cm = split_last(xbc, (SSD_INNER, SSD_GN, SSD_GN))
    y_ssd, s_f, s_b = ssd_bidirectional(xs, Bm, Cm, pdt, dt_bias, a_log, d_skip, h0_f, h0_b)
    y_ssd = gated_norm(y_ssd, pz, ssd_norm_w) @ ssd_out
    q = pq.reshape(b, n, DIFF_HEADS, 2, DIFF_HEAD_DIM)
    k = pk.reshape(b, n, DIFF_HEADS, 2, DIFF_HEAD_DIM)
    v = pv.reshape(b, n, DIFF_HEADS, 2 * DIFF_HEAD_DIM)
    if rope is not None:
        q = apply_axial_rope(q, rope[0], rope[1])
        k = apply_axial_rope(k, rope[0], rope[1])
    if kv_prefix is None:
        k_all, v_all = k, v
    else:
        k_all = jnp.concatenate([kv_prefix[0], k], axis=1)
        v_all = jnp.concatenate([kv_prefix[1], v], axis=1)
    o = diff_attention(q, k_all, v_all, lam)
    y_diff = (rms_norm(o, subln_w) * (1 - lam_init)).reshape(b, n, DIFF_WIDTH) @ diff_out
    y_conf = conformer_conv(pglu, dw_w, dw_b, ln_w, ln_b, conf_out)
    return gated_merge(pgate, y_ssd, y_diff, y_conf, w_o), k, v, s_f, s_b


def context_state(h, w_in, conv_w, conv_b, dt_bias, a_log):
    b, n = h.shape[0], h.shape[1]
    px, pB, pdt, pk, pv = split_last(h @ w_in[:, :N_STATE_COLS], COL_SIZES[:5])
    nxb = SSD_INNER + SSD_GN
    xb = jax.nn.silu(dwconv(jnp.concatenate([px, pB], axis=-1), conv_w[:, :nxb], conv_b[:nxb]))
    xs, Bm = split_last(xb, (SSD_INNER, SSD_GN))
    xs = xs.reshape(b, n, SSD_HEADS, SSD_HEAD_DIM)
    Bh = expand_groups(Bm)
    dt, a = ssd_steps(pdt, dt_bias, a_log)
    s_f = ssd_final_state(xs * dt[:, :, 0, :, None], a[:, :, 0], Bh)
    s_b = ssd_final_state(flip_seq(xs * dt[:, :, 1, :, None]), flip_seq(a[:, :, 1]), flip_seq(Bh))
    k = pk.reshape(b, n, DIFF_HEADS, 2, DIFF_HEAD_DIM)
    v = pv.reshape(b, n, DIFF_HEADS, 2 * DIFF_HEAD_DIM)
    return k, v, s_f, s_b


def expert_choice_ffn(h, router_w, w1, w3, w2):
    n = h.shape[1]
    cap = CAPACITY_FACTOR * n // N_EXPERTS

    def route_set(hs):
        aff = jax.nn.softmax((hs @ router_w).astype(jnp.float32), axis=-1)
        g, idx = lax.top_k(aff.T, cap)
        xe = hs[idx]
        he = jax.nn.silu(jnp.einsum('ecd,edf->ecf', xe, w1)) * jnp.einsum('ecd,edf->ecf', xe, w3)
        ye = jnp.einsum('ecf,efd->ecd', he, w2) * g[..., None].astype(hs.dtype)
        return jnp.zeros_like(hs).at[idx.reshape(-1)].add(ye.reshape(-1, hs.shape[-1]).astype(hs.dtype))

    return jax.vmap(route_set)(h)


def setup_inputs(seed: int = 0) -> dict:
    key = jax.random.key(seed)
    keys = jax.random.split(key, 40)
    order = iter(range(40))

    def normal(shape, scale):
        return jax.random.normal(keys[next(order)], shape, jnp.float32) * scale

    def gain(shape):
        return 1.0 + normal(shape, 0.02)

    L, D, E, F = DEPTH, D_MODEL, N_EXPERTS, EXPERT_FF
    conv_ch = SSD_INNER + 2 * SSD_GN
    dt0 = jnp.exp(jax.random.uniform(keys[next(order)], (L, 2, SSD_HEADS), jnp.float32, minval=math.log(1e-3), maxval=math.log(1e-1)))
    a0 = jax.random.uniform(keys[next(order)], (L, 2, SSD_HEADS), jnp.float32, minval=1.0, maxval=16.0)
    return {
        'x': normal((BATCH, SEQ, D), 1.0),
        'c': normal((BATCH, D), 1.0),
        'ctx': normal((BATCH, CTX_LEN, D), 1.0),
        'c_ctx': normal((D,), 1.0),
        'ada_w': normal((L, D, N_MOD * D), 0.5 * D ** -0.5),
        'ada_b': normal((L, N_MOD * D), 0.02),
        'norm1_w': gain((L, D)),
        'norm2_w': gain((L, D)),
        'w_in': normal((L, D, N_COLS), D ** -0.5),
        'ssd_conv_w': normal((L, SSD_CONV, conv_ch), SSD_CONV ** -0.5),
        'ssd_conv_b': normal((L, conv_ch), 0.02),
        'ssd_dt_bias': dt0 + jnp.log(-jnp.expm1(-dt0)),
        'ssd_a_log': jnp.log(a0),
        'ssd_d': gain((L, SSD_HEADS)),
        'ssd_norm_w': gain((L, SSD_INNER)),
        'ssd_out': normal((L, SSD_INNER, D), SSD_INNER ** -0.5),
        'diff_lambda': normal((L, 4, DIFF_HEAD_DIM), 0.1),
        'diff_subln_w': gain((L, 2 * DIFF_HEAD_DIM)),
        'diff_out': normal((L, DIFF_WIDTH, D), DIFF_WIDTH ** -0.5),
        'conf_dw_w': normal((L, CONF_KERNEL, CONF_WIDTH), CONF_KERNEL ** -0.5),
        'conf_dw_b': normal((L, CONF_WIDTH), 0.02),
        'conf_ln_w': gain((L, CONF_WIDTH)),
        'conf_ln_b': normal((L, CONF_WIDTH), 0.02),
        'conf_out': normal((L, CONF_WIDTH, D), CONF_WIDTH ** -0.5),
        'w_o': normal((L, D, D), D ** -0.5),
        'router_w': normal((L, D, E), D ** -0.5),
        'exp_w1': normal((L, E, D, F), D ** -0.5),
        'exp_w3': normal((L, E, D, F), D ** -0.5),
        'exp_w2': normal((L, E, F, D), F ** -0.5),
        'final_norm_w': gain((D,)),
    }


def reference(x, c, ctx, c_ctx, ada_w, ada_b, norm1_w, norm2_w, w_in, ssd_conv_w, ssd_conv_b, ssd_dt_bias, ssd_a_log, ssd_d, ssd_norm_w, ssd_out, diff_lambda, diff_subln_w, diff_out, conf_dw_w, conf_dw_b, conf_ln_w, conf_ln_b, conf_out, w_o, router_w, exp_w1, exp_w3, exp_w2, final_norm_w):
    b = x.shape[0]
    n_rows = x.shape[1] // GRID_W
    rope = axial_rope_tables(n_rows)
    x_lat, x_ctx = x, ctx
    for l in range(DEPTH):
        last = l == DEPTH - 1
        lam_init = 0.8 - 0.6 * math.exp(-0.3 * l)
        lq1, lk1, lq2, lk2 = diff_lambda[l].astype(jnp.float32)
        lam = jnp.exp(jnp.sum(lq1 * lk1)) - jnp.exp(jnp.sum(lq2 * lk2)) + lam_init
        m_lat = jnp.split((jax.nn.silu(c) @ ada_w[l] + ada_b[l])[:, None, :], N_MOD, axis=-1)
        m_ctx = jnp.split((jax.nn.silu(c_ctx) @ ada_w[l] + ada_b[l])[None, None, :], N_MOD, axis=-1)
        mp = (ssd_conv_w[l], ssd_conv_b[l], ssd_dt_bias[l], ssd_a_log[l], ssd_d[l], ssd_norm_w[l], ssd_out[l], diff_subln_w[l], diff_out[l], conf_dw_w[l], conf_dw_b[l], conf_ln_w[l], conf_ln_b[l], conf_out[l], w_o[l])
        h_c = modulate(x_ctx, norm1_w[l], m_ctx[0], m_ctx[1])
        if last:
            k_c, v_c, s_f, s_b = context_state(h_c, w_in[l], ssd_conv_w[l], ssd_conv_b[l], ssd_dt_bias[l], ssd_a_log[l])
        else:
            zero = jnp.zeros((b, SSD_HEADS, SSD_HEAD_DIM, SSD_STATE), jnp.float32)
            y_c, k_c, v_c, s_f, s_b = token_mixer(h_c, w_in[l], mp, zero, zero, None, None, lam, lam_init)
            x_ctx = x_ctx + (m_ctx[2] * y_c).astype(x_ctx.dtype)
        h_l = modulate(x_lat, norm1_w[l], m_lat[0], m_lat[1])
        y_l = token_mixer(h_l, w_in[l], mp, s_f, s_b, (k_c, v_c), rope, lam, lam_init)[0]
        x_lat = x_lat + (m_lat[2] * y_l).astype(x_lat.dtype)
        h2 = modulate(x_lat, norm2_w[l], m_lat[3], m_lat[4])
        x_lat = x_lat + (m_lat[5] * expert_choice_ffn(h2, router_w[l], exp_w1[l], exp_w3[l], exp_w2[l])).astype(x_lat.dtype)
        if not last:
            h2c = modulate(x_ctx, norm2_w[l], m_ctx[3], m_ctx[4])
            x_ctx = x_ctx + (m_ctx[5] * expert_choice_ffn(h2c, router_w[l], exp_w1[l], exp_w3[l], exp_w2[l])).astype(x_ctx.dtype)
    return rms_norm(x_lat, final_norm_w)
```

```python
import functools
import math

import jax
import jax.numpy as jnp
import numpy as np
from jax import lax
from jax.experimental import pallas as pl
from jax.experimental.pallas import tpu as pltpu

F32 = jnp.float32
BF16 = jnp.bfloat16
I32 = jnp.int32

EPS = 1e-6
D = 1024
N_MOD = 6
GRID_W = 64
SSD_HEADS = 16
SSD_HEAD_DIM = 64
SSD_STATE = 128
SSD_GN = 256
SSD_CHUNK = 128
DIFF_HEADS = 8
DIFF_HEAD_DIM = 64
ROPE_BASE = 10000.0
CONF_KERNEL = 31
CONF_HALO = 16
N_EXPERTS = 16
EXPERT_FF = 2048
CAPACITY_FACTOR = 2
LANES = 128

OFF_X, OFF_Z, OFF_K, OFF_V, OFF_Q = 0, 1024, 2048, 3072, 4096
OFF_GLU_A, OFF_GLU_G, OFF_GATE = 5120, 6144, 7168
OFF_B, OFF_C, OFF_DT = 10240, 10496, 10752
N_PROJ = 11264

VMEM_LIMIT = 56 * 1024 * 1024


def _params(sem, vmem=VMEM_LIMIT):
    return pltpu.CompilerParams(dimension_semantics=sem, vmem_limit_bytes=vmem)


def _silu(x):
    return x * jax.nn.sigmoid(x)


def _split3(x):
    a1 = x.astype(BF16)
    r1 = x - a1.astype(F32)
    a2 = r1.astype(BF16)
    a3 = (r1 - a2.astype(F32)).astype(BF16)
    return a1, a2, a3


def _dot(a, b):
    return jnp.dot(a, b, preferred_element_type=F32)


def _ada_kernel(c_ref, w_ref, b_ref, o_ref):
    a = _silu(c_ref[...]).astype(BF16)
    o_ref[0] = _dot(a, w_ref[0].astype(BF16)) + b_ref[0]


def _ada(cc, ada_w, ada_b):
    nl = ada_w.shape[0]
    tn = 1536
    return pl.pallas_call(
        _ada_kernel,
        grid=(nl, N_MOD * D // tn),
        in_specs=[
            pl.BlockSpec((8, D), lambda l, j: (0, 0)),
            pl.BlockSpec((1, D, tn), lambda l, j: (l, 0, j)),
            pl.BlockSpec((1, 1, tn), lambda l, j: (l, 0, j)),
        ],
        out_specs=pl.BlockSpec((1, 8, tn), lambda l, j: (l, 0, j)),
        out_shape=jax.ShapeDtypeStruct((nl, 8, N_MOD * D), F32),
        compiler_params=_params(("arbitrary", "arbitrary")),
        name="ada",
    )(cc, ada_w, ada_b.reshape(nl, 1, N_MOD * D))


def _modulate(x, nw, sh, sc):
    r = lax.rsqrt(jnp.mean(x * x, axis=-1, keepdims=True) + EPS)
    return (x * r * nw) * (1.0 + sc) + sh


def _inproj_kernel(x_ref, nw_ref, sh_ref, sc_ref, w_ref, o_ref, h_scr):
    @pl.when(pl.program_id(1) == 0)
    def _():
        h_scr[...] = _modulate(x_ref[...], nw_ref[...], sh_ref[...], sc_ref[...]).astype(BF16)

    o_ref[...] = _dot(h_scr[...], w_ref[...])


def _inproj(x, nw, sh, sc, w):
    n = x.shape[0]
    tm = min(n, 1024)
    tn = 1024
    vec = pl.BlockSpec((1, D), lambda i, j: (0, 0))
    return pl.pallas_call(
        _inproj_kernel,
        grid=(n // tm, N_PROJ // tn),
        in_specs=[pl.BlockSpec((tm, D), lambda i, j: (i, 0)), vec, vec, vec,
                  pl.BlockSpec((D, tn), lambda i, j: (0, j))],
        out_specs=pl.BlockSpec((tm, tn), lambda i, j: (i, j)),
        out_shape=jax.ShapeDtypeStruct((n, N_PROJ), F32),
        scratch_shapes=[pltpu.VMEM((tm, D), BF16)],
        compiler_params=_params(("arbitrary", "arbitrary")),
        name="inproj",
    )(x, nw, sh, sc, w)


def _ssdconv_kernel(prev_ref, cur_ref, next_ref, w_ref, b_ref, o_ref):
    i = pl.program_id(0)
    nt = pl.num_programs(0)
    u = cur_ref[...]
    t = u.shape[0]
    row = lax.broadcasted_iota(I32, u.shape, 0)
    before = jnp.where(i > 0, prev_ref[7:8, :], 0.0)
    after = jnp.where(i < nt - 1, next_ref[0:1, :], 0.0)
    um1 = jnp.where(row == 0, before, pltpu.roll(u, 1, 0))
    up1 = jnp.where(row == t - 1, after, pltpu.roll(u, t - 1, 0))
    y = um1 * w_ref[0:1, :] + u * w_ref[1:2, :] + up1 * w_ref[2:3, :] + b_ref[...]
    o_ref[...] = _silu(y)


def _ssdconv(p, conv_w, conv_b):
    n = p.shape[0]
    t = min(n, 512)
    cw = 512
    r8 = t // 8
    nb8 = n // 8

    def col(c):
        return jnp.where(c < 2, c, OFF_B // cw)

    return pl.pallas_call(
        _ssdconv_kernel,
        grid=(n // t, 3),
        in_specs=[
            pl.BlockSpec((8, cw), lambda i, c: (jnp.maximum(i * r8 - 1, 0), col(c))),
            pl.BlockSpec((t, cw), lambda i, c: (i, col(c))),
            pl.BlockSpec((8, cw), lambda i, c: (jnp.minimum((i + 1) * r8, nb8 - 1), col(c))),
            pl.BlockSpec((3, cw), lambda i, c: (0, c)),
            pl.BlockSpec((1, cw), lambda i, c: (0, c)),
        ],
        out_specs=pl.BlockSpec((t, cw), lambda i, c: (i, c)),
        out_shape=jax.ShapeDtypeStruct((n, 3 * cw), F32),
        compiler_params=_params(("arbitrary", "arbitrary")),
        name="ssdconv",
    )(p, p, p, conv_w, conv_b.reshape(1, -1))


def _ssd_kernel(xs_ref, b_ref, c_ref, dt_ref, dtb_ref, alog_ref, h0_ref, y_ref, hT_ref, st_scr):
    d = pl.program_id(0)
    c = pl.program_id(1)
    nc = pl.num_programs(1)
    L = SSD_CHUNK
    P = SSD_HEAD_DIM
    HP = SSD_HEADS * P
    GW = HP // 2

    @pl.when(c == 0)
    def _():
        st_scr[...] = h0_ref[0]

    fwd = d == 0
    x = dt_ref[...] + dtb_ref[0]
    dt = jnp.maximum(x, 0.0) + jnp.log1p(jnp.exp(-jnp.abs(x)))
    a = dt * (-jnp.exp(alog_ref[0]))

    li = lax.broadcasted_iota(I32, (L, L), 0)
    si = lax.broadcasted_iota(I32, (L, L), 1)
    keep = (li - si) * jnp.where(fwd, 1, -1) >= 0
    tri = jnp.where(keep, 1.0, 0.0).astype(BF16)
    a1, a2, a3 = _split3(a)
    cum = _dot(tri, a1) + _dot(tri, a2) + _dot(tri, a3)
    cum_t = cum.T

    ex = jnp.where(lax.broadcasted_iota(I32, (LANES, HP), 0) == lax.broadcasted_iota(I32, (LANES, HP), 1) // P,
                   1.0, 0.0).astype(BF16)

    def expand(v):
        v1, v2, v3 = _split3(v)
        return _dot(v1, ex) + _dot(v2, ex) + _dot(v3, ex)

    cum_e = expand(cum)
    dt_e = expand(dt)
    tot_e = jnp.where(fwd, cum_e[L - 1:L, :], cum_e[0:1, :])

    xdt = xs_ref[...] * dt_e
    xb = xdt.astype(BF16)
    xw = (xdt * jnp.exp(tot_e - cum_e)).astype(BF16)
    bb = b_ref[...].astype(BF16)
    cb = c_ref[...].astype(BF16)
    st = st_scr[...]
    stb = st.astype(BF16)
    grow = jnp.exp(cum_e)

    for g in range(2):
        bg = bb[:, g * SSD_STATE:(g + 1) * SSD_STATE]
        cg = cb[:, g * SSD_STATE:(g + 1) * SSD_STATE]
        scores = lax.dot_general(cg, bg, (((1,), (1,)), ((), ())), preferred_element_type=F32)
        y_off = _dot(cg, stb[:, g * GW:(g + 1) * GW]) * grow[:, g * GW:(g + 1) * GW]
        for hh in range(SSD_HEADS // 2):
            h = g * (SSD_HEADS // 2) + hh
            seg = cum[:, h:h + 1] - cum_t[h:h + 1, :]
            dec = jnp.where(keep, jnp.exp(jnp.where(keep, seg, 0.0)), 0.0)
            m = (scores * dec).astype(BF16)
            yd = _dot(m, xb[:, h * P:(h + 1) * P])
            y_ref[0, :, h * P:(h + 1) * P] = yd + y_off[:, hh * P:(hh + 1) * P]
        upd = lax.dot_general(bg, xw[:, g * GW:(g + 1) * GW], (((0,), (0,)), ((), ())),
                              preferred_element_type=F32)
        st_scr[:, g * GW:(g + 1) * GW] = st[:, g * GW:(g + 1) * GW] * jnp.exp(tot_e[:, g * GW:(g + 1) * GW]) + upd

    @pl.when(c == nc - 1)
    def _():
        hT_ref[0] = st_scr[...]


def _ssd(xbc, p, dtb, alog, h0):
    n = xbc.shape[0]
    L = SSD_CHUNK
    nc = n // L
    HP = SSD_HEADS * SSD_HEAD_DIM

    def chunk(d, c):
        return c + d * (nc - 1 - 2 * c)

    return pl.pallas_call(
        _ssd_kernel,
        grid=(2, nc),
        in_specs=[
            pl.BlockSpec((L, HP), lambda d, c: (chunk(d, c), 0)),
            pl.BlockSpec((L, SSD_GN), lambda d, c: (chunk(d, c), HP // SSD_GN)),
            pl.BlockSpec((L, SSD_GN), lambda d, c: (chunk(d, c), HP // SSD_GN + 1)),
            pl.BlockSpec((L, LANES), lambda d, c: (chunk(d, c), OFF_DT // LANES + d)),
            pl.BlockSpec((1, 1, LANES), lambda d, c: (d, 0, 0)),
            pl.BlockSpec((1, 1, LANES), lambda d, c: (d, 0, 0)),
            pl.BlockSpec((1, SSD_STATE, HP), lambda d, c: (d, 0, 0)),
        ],
        out_specs=[
            pl.BlockSpec((1, L, HP), lambda d, c: (d, chunk(d, c), 0)),
            pl.BlockSpec((1, SSD_STATE, HP), lambda d, c: (d, 0, 0)),
        ],
        out_shape=[jax.ShapeDtypeStruct((2, n, HP), F32), jax.ShapeDtypeStruct((2, SSD_STATE, HP), F32)],
        scratch_shapes=[pltpu.VMEM((SSD_STATE, HP), F32)],
        compiler_params=_params(("arbitrary", "arbitrary")),
        name="ssd",
    )(xbc, xbc, xbc, p, dtb, alog, h0)


def _rope_tile(x_ref, o_ref, cos, sa, sb, scale):
    for j in range(D // LANES):
        xj = x_ref[:, j * LANES:(j + 1) * LANES]
        yj = xj * cos + pltpu.roll(xj, LANES - 16, 1) * sa + pltpu.roll(xj, 16, 1) * sb
        o_ref[:, j * LANES:(j + 1) * LANES] = (yj * scale).astype(o_ref.dtype)


def _qprep_kernel(q_ref, cos_ref, sa_ref, sb_ref, o_ref, *, scale):
    _rope_tile(q_ref, o_ref, cos_ref[...], sa_ref[...], sb_ref[...], scale)


def _qprep(p, tabs, scale):
    n = p.shape[0]
    t = min(n, 256)
    tab = pl.BlockSpec((t, LANES), lambda i: (i, 0))
    return pl.pallas_call(
        functools.partial(_qprep_kernel, scale=scale),
        grid=(n // t,),
        in_specs=[pl.BlockSpec((t, D), lambda i: (i, OFF_Q // D)), tab, tab, tab],
        out_specs=pl.BlockSpec((t, D), lambda i: (i, 0)),
        out_shape=jax.ShapeDtypeStruct((n, D), BF16),
        compiler_params=_params(("arbitrary",)),
        name="qprep",
    )(p, *tabs)


def _kvprep_kernel(kc_ref, vc_ref, kl_ref, vl_ref, cos_ref, sa_ref, sb_ref, ko_ref, vo_ref):
    i = pl.program_id(0)

    @pl.when(i == 0)
    def _():
        ko_ref[...] = kc_ref[...].astype(BF16)
        vo_ref[...] = vc_ref[...].astype(BF16)

    @pl.when(i > 0)
    def _():
        _rope_tile(kl_ref, ko_ref, cos_ref[...], sa_ref[...], sb_ref[...], 1.0)
        vo_ref[...] = vl_ref[...].astype(BF16)


def _kvprep(p_ctx, p_lat, tabs):
    nctx = p_ctx.shape[0]
    n = p_lat.shape[0]
    t = nctx
    m = nctx + n

    def lat(i):
        return jnp.maximum(i - 1, 0)

    tab = pl.BlockSpec((t, LANES), lambda i: (lat(i), 0))
    out = pl.BlockSpec((t, D), lambda i: (i, 0))
    return pl.pallas_call(
        _kvprep_kernel,
        grid=(m // t,),
        in_specs=[
            pl.BlockSpec((t, D), lambda i: (0, OFF_K // D)),
            pl.BlockSpec((t, D), lambda i: (0, OFF_V // D)),
            pl.BlockSpec((t, D), lambda i: (lat(i), OFF_K // D)),
            pl.BlockSpec((t, D), lambda i: (lat(i), OFF_V // D)),
            tab, tab, tab,
        ],
        out_specs=[out, out],
        out_shape=[jax.ShapeDtypeStruct((m, D), BF16), jax.ShapeDtypeStruct((m, D), BF16)],
        compiler_params=_params(("arbitrary",)),
        name="kvprep",
    )(p_ctx, p_ctx, p_lat, p_lat, *tabs)


def _attn_kernel(lam_ref, q_ref, k_ref, v_ref, sw_ref, o_ref, q2_scr, m_scr, l_scr, acc_scr, *, tq, coef):
    ki = pl.program_id(2)
    nk = pl.num_programs(2)

    @pl.when(ki == 0)
    def _():
        q = q_ref[...]
        lane = lax.broadcasted_iota(I32, q.shape, 1)
        zero = jnp.zeros_like(q)
        q2_scr[0:tq, :] = jnp.where(lane < DIFF_HEAD_DIM, q, zero)
        q2_scr[tq:2 * tq, :] = jnp.where(lane >= DIFF_HEAD_DIM, q, zero)
        m_scr[...] = jnp.full(m_scr.shape, -jnp.inf, F32)
        l_scr[...] = jnp.zeros(l_scr.shape, F32)
        acc_scr[...] = jnp.zeros(acc_scr.shape, F32)

    s = lax.dot_general(q2_scr[...], k_ref[...], (((1,), (1,)), ((), ())), preferred_element_type=F32)
    m_prev = m_scr[...]
    m_new = jnp.maximum(m_prev, jnp.max(s, axis=1, keepdims=True))
    alpha = jnp.exp(m_prev - m_new)
    p = jnp.exp(s - m_new[:, 0:1])
    l_scr[...] = alpha * l_scr[...] + jnp.sum(p, axis=1, keepdims=True)
    acc_scr[...] = alpha * acc_scr[...] + _dot(p.astype(BF16), v_ref[...])
    m_scr[...] = m_new

    @pl.when(ki == nk - 1)
    def _():
        on = acc_scr[...] / l_scr[...]
        o = on[0:tq, :] - lam_ref[0] * on[tq:2 * tq, :]
        r = lax.rsqrt(jnp.mean(o * o, axis=-1, keepdims=True) + EPS)
        o_ref[...] = ((o * r * sw_ref[...]) * coef).astype(o_ref.dtype)


def _pick_tk(m):
    for cand in (1280, 1024, 512, 256):
        if m % cand == 0:
            return cand
    return m


def _attention(lam, q, k, v, subln_w, m, coef):
    n = q.shape[0]
    tq = min(n, 256)
    tk = _pick_tk(m)
    return pl.pallas_call(
        functools.partial(_attn_kernel, tq=tq, coef=coef),
        grid=(DIFF_HEADS, n // tq, m // tk),
        in_specs=[
            pl.BlockSpec(memory_space=pltpu.SMEM),
            pl.BlockSpec((tq, LANES), lambda h, i, j: (i, h)),
            pl.BlockSpec((tk, LANES), lambda h, i, j: (j, h)),
            pl.BlockSpec((tk, LANES), lambda h, i, j: (j, h)),
            pl.BlockSpec((1, LANES), lambda h, i, j: (0, 0)),
        ],
        out_specs=pl.BlockSpec((tq, LANES), lambda h, i, j: (i, h)),
        out_shape=jax.ShapeDtypeStruct((n, D), BF16),
        scratch_shapes=[pltpu.VMEM((2 * tq, LANES), BF16), pltpu.VMEM((2 * tq, LANES), F32),
                        pltpu.VMEM((2 * tq, LANES), F32), pltpu.VMEM((2 * tq, LANES), F32)],
        compiler_params=_params(("arbitrary", "arbitrary", "arbitrary")),
        name="attn",
    )(lam, q, k, v, subln_w)


def _conf_kernel(ap_ref, gp_ref, a_ref, g_ref, an_ref, gn_ref, w_ref, b_ref, lw_ref, lb_ref, o_ref, ext_scr):
    i = pl.program_id(0)
    nt = pl.num_programs(0)
    t = a_ref.shape[0]
    hl = CONF_HALO

    def glu(a, g):
        return a[...] * jax.nn.sigmoid(g[...])

    ext_scr[0:hl, :] = jnp.where(i > 0, glu(ap_ref, gp_ref), 0.0)
    ext_scr[hl:hl + t, :] = glu(a_ref, g_ref)
    ext_scr[hl + t:2 * hl + t, :] = jnp.where(i < nt - 1, glu(an_ref, gn_ref), 0.0)

    acc = jnp.zeros((t, D), F32) + b_ref[...]
    for k in range(CONF_KERNEL):
        acc = acc + ext_scr[pl.ds(k + hl - CONF_KERNEL // 2, t), :] * w_ref[k:k + 1, :]
    mu = jnp.mean(acc, axis=-1, keepdims=True)
    xc = acc - mu
    y = xc * lax.rsqrt(jnp.mean(xc * xc, axis=-1, keepdims=True) + EPS) * lw_ref[...] + lb_ref[...]
    o_ref[...] = _silu(y).astype(o_ref.dtype)


def _conformer(p, dw_w, dw_b, ln_w, ln_b):
    n = p.shape[0]
    t = min(n, 256)
    hl = CONF_HALO
    rh = t // hl
    nbh = n // hl
    ca, cg = OFF_GLU_A // D, OFF_GLU_G // D

    def prev(i):
        return jnp.maximum(i * rh - 1, 0)

    def nxt(i):
        return jnp.minimum((i + 1) * rh, nbh - 1)

    vec = pl.BlockSpec((1, D), lambda i: (0, 0))
    return pl.pallas_call(
        _conf_kernel,
        grid=(n // t,),
        in_specs=[
            pl.BlockSpec((hl, D), lambda i: (prev(i), ca)), pl.BlockSpec((hl, D), lambda i: (prev(i), cg)),
            pl.BlockSpec((t, D), lambda i: (i, ca)), pl.BlockSpec((t, D), lambda i: (i, cg)),
            pl.BlockSpec((hl, D), lambda i: (nxt(i), ca)), pl.BlockSpec((hl, D), lambda i: (nxt(i), cg)),
            pl.BlockSpec((CONF_KERNEL, D), lambda i: (0, 0)), vec, vec, vec,
        ],
        out_specs=pl.BlockSpec((t, D), lambda i: (i, 0)),
        out_shape=jax.ShapeDtypeStruct((n, D), BF16),
        scratch_shapes=[pltpu.VMEM((t + 2 * hl, D), F32)],
        compiler_params=_params(("arbitrary",)),
        name="conformer",
    )(p, p, p, p, p, p, dw_w, dw_b.reshape(1, D), ln_w.reshape(1, D), ln_b.reshape(1, D))


def _merge_kernel(yf_ref, yb_ref, xs_ref, z_ref, on_ref, cn_ref, g0_ref, g1_ref, g2_ref, x_ref, dsk_ref, nw_ref,
                  m2_ref, wa_ref, wb_ref, wc_ref, wo_ref, o_ref):
    y = yf_ref[0] + yb_ref[0] + xs_ref[...] * dsk_ref[...]
    yz = y * _silu(z_ref[...])
    half = D // 2
    parts = []
    for g in range(2):
        seg = yz[:, g * half:(g + 1) * half]
        r = lax.rsqrt(jnp.mean(seg * seg, axis=-1, keepdims=True) + EPS)
        parts.append(seg * r * nw_ref[:, g * half:(g + 1) * half])
    gn = jnp.concatenate(parts, axis=1).astype(BF16)
    y_a = _dot(gn, wa_ref[...])
    y_b = _dot(on_ref[...], wb_ref[...])
    y_c = _dot(cn_ref[...], wc_ref[...])
    m = jax.nn.sigmoid(g0_ref[...]) * y_a + jax.nn.sigmoid(g1_ref[...]) * y_b + jax.nn.sigmoid(g2_ref[...]) * y_c
    y_o = _dot(m.astype(BF16), wo_ref[...])
    o_ref[...] = x_ref[...] + m2_ref[...] * y_o


def _merge(y2, xbc, p, on, cn, x, dsk, nw, m2, wa, wb, wc, wo):
    n = x.shape[0]
    tm = min(n, 256)
    vec = pl.BlockSpec((1, D), lambda i: (0, 0))
    wsp = pl.BlockSpec((D, D), lambda i: (0, 0))
    row = pl.BlockSpec((tm, D), lambda i: (i, 0))
    return pl.pallas_call(
        _merge_kernel,
        grid=(n // tm,),
        in_specs=[
            pl.BlockSpec((1, tm, D), lambda i: (0, i, 0)),
            pl.BlockSpec((1, tm, D), lambda i: (1, i, 0)),
            row,
            pl.BlockSpec((tm, D), lambda i: (i, OFF_Z // D)),
            row, row,
            pl.BlockSpec((tm, D), lambda i: (i, OFF_GATE // D)),
            pl.BlockSpec((tm, D), lambda i: (i, OFF_GATE // D + 1)),
            pl.BlockSpec((tm, D), lambda i: (i, OFF_GATE // D + 2)),
            row, vec, vec, vec, wsp, wsp, wsp, wsp,
        ],
        out_specs=row,
        out_shape=jax.ShapeDtypeStruct((n, D), F32),
        compiler_params=_params(("arbitrary",)),
        name="merge",
    )(y2, y2, xbc, p, on, cn, p, p, p, x, dsk, nw, m2, wa, wb, wc, wo)


def _router_kernel(x_ref, nw_ref, sh_ref, sc_ref, rw_ref, h_ref, aff_ref, afft_ref):
    h = _modulate(x_ref[...], nw_ref[...], sh_ref[...], sc_ref[...]).astype(BF16)
    h_ref[...] = h
    logits = _dot(h, rw_ref[...])
    lane = lax.broadcasted_iota(I32, logits.shape, 1)
    logits = jnp.where(lane < N_EXPERTS, logits, -jnp.inf)
    e = jnp.exp(logits - jnp.max(logits, axis=-1, keepdims=True))
    aff = e / jnp.sum(e, axis=-1, keepdims=True)
    aff_ref[...] = aff
    afft_ref[...] = aff.T[0:N_EXPERTS, :]


def _router(x, nw, sh, sc, rw):
    n = x.shape[0]
    tm = min(n, 512)
    vec = pl.BlockSpec((1, D), lambda i: (0, 0))
    return pl.pallas_call(
        _router_kernel,
        grid=(n // tm,),
        in_specs=[pl.BlockSpec((tm, D), lambda i: (i, 0)), vec, vec, vec, pl.BlockSpec((D, LANES), lambda i: (0, 0))],
        out_specs=[pl.BlockSpec((tm, D), lambda i: (i, 0)), pl.BlockSpec((tm, LANES), lambda i: (i, 0)),
                   pl.BlockSpec((N_EXPERTS, tm), lambda i: (0, i))],
        out_shape=[jax.ShapeDtypeStruct((n, D), BF16), jax.ShapeDtypeStruct((n, LANES), F32),
                   jax.ShapeDtypeStruct((N_EXPERTS, n), F32)],
        compiler_params=_params(("arbitrary",)),
        name="router",
    )(x, nw, sh, sc, rw)


def _select_kernel(a_ref, pose_ref, post_ref, off_ref, *, cap):
    n = a_ref.shape[1]
    nb = n // LANES
    def search(it, thr):
        bits = pltpu.bitcast(a_ref[...], I32)
        cand = thr | lax.shift_left(jnp.int32(1), 30 - it)
        cnt = jnp.sum(jnp.where(bits >= cand, 1.0, 0.0), axis=1, keepdims=True)
        return jnp.where(cnt >= cap, cand, thr)

    thr = lax.fori_loop(0, 31, search, jnp.zeros((N_EXPERTS, 1), I32))
    n_gt = jnp.sum(jnp.where(pltpu.bitcast(a_ref[...], I32) > thr, 1.0, 0.0), axis=1, keepdims=True)
    need = cap - n_gt

    tri = jnp.where(lax.broadcasted_iota(I32, (LANES, LANES), 0) <= lax.broadcasted_iota(I32, (LANES, LANES), 1),
                    1.0, 0.0).astype(BF16)

    def block(b, carry):
        c_eq, c_sel = carry
        start = pl.multiple_of(b * LANES, LANES)
        bb = pltpu.bitcast(a_ref[:, pl.ds(start, LANES)], I32)
        gt = jnp.where(bb > thr, 1.0, 0.0)
        eq = jnp.where(bb == thr, 1.0, 0.0)
        inc_eq = _dot(eq.astype(BF16), tri)
        rank_eq = c_eq + inc_eq - eq
        sel = gt + eq * jnp.where(rank_eq < need, 1.0, 0.0)
        inc_sel = _dot(sel.astype(BF16), tri)
        pos = jnp.where(sel > 0.0, c_sel + inc_sel - sel, -1.0)
        pose_ref[:, pl.ds(start, LANES)] = pos.astype(I32)
        full = jnp.concatenate([pos, jnp.full((LANES - N_EXPERTS, LANES), -1.0, F32)], axis=0)
        post_ref[pl.ds(start, LANES), :] = full.T
        off_ref[b] = jnp.broadcast_to(c_sel, (N_EXPERTS, LANES))
        return c_eq + inc_eq[:, LANES - 1:LANES], c_sel + inc_sel[:, LANES - 1:LANES]

    zero = jnp.zeros((N_EXPERTS, 1), F32)
    lax.fori_loop(0, nb, block, (zero, zero))


def _select(afft, cap):
    n = afft.shape[1]
    nb = n // LANES
    return pl.pallas_call(
        functools.partial(_select_kernel, cap=cap),
        out_shape=[jax.ShapeDtypeStruct((N_EXPERTS, n), I32), jax.ShapeDtypeStruct((n, LANES), F32),
                   jax.ShapeDtypeStruct((nb, N_EXPERTS, LANES), F32)],
        compiler_params=_params(None),
        name="select",
    )(afft)


def _gather_kernel(jt_ref, tt_ref, fl_ref, pos_ref, h_ref, o_ref, *, sb, kp):
    e = pl.program_id(0)
    k = pl.program_id(1)
    idx = e * kp + k
    j = jt_ref[idx]
    flags = fl_ref[idx]

    @pl.when((flags & 2) != 0)
    def _():
        o_ref[...] = jnp.zeros(o_ref.shape, o_ref.dtype)

    @pl.when((flags & 1) != 0)
    def _():
        pos = pos_ref[0]
        tb = pos.shape[1]
        slot = lax.broadcasted_iota(I32, (sb, tb), 0) + j * sb
        onehot = jnp.where(pos == slot, 1.0, 0.0).astype(BF16)
        o_ref[0] = (o_ref[0].astype(F32) + _dot(onehot, h_ref[...])).astype(o_ref.dtype)


def _gather(jt, tt, fl, pose, h2, cap, sb, tb, kp):
    n = h2.shape[0]
    nt = n // tb
    pos3 = pose.reshape(N_EXPERTS * nt, 1, tb)
    grid_spec = pltpu.PrefetchScalarGridSpec(
        num_scalar_prefetch=3,
        grid=(N_EXPERTS, kp),
        in_specs=[
            pl.BlockSpec((1, 1, tb), lambda e, k, jt, tt, fl: (e * nt + tt[e * kp + k], 0, 0)),
            pl.BlockSpec((tb, D), lambda e, k, jt, tt, fl: (tt[e * kp + k], 0)),
        ],
        out_specs=pl.BlockSpec((1, sb, D), lambda e, k, jt, tt, fl: (e, jt[e * kp + k], 0)),
    )
    return pl.pallas_call(
        functools.partial(_gather_kernel, sb=sb, kp=kp),
        grid_spec=grid_spec,
        out_shape=jax.ShapeDtypeStruct((N_EXPERTS, cap, D), BF16),
        compiler_params=_params(("arbitrary", "arbitrary")),
        name="gather",
    )(jt, tt, fl, pos3, h2)


def _ffn_kernel(x_ref, w1_ref, w3_ref, w2_ref, o_ref, acc_scr):
    f = pl.program_id(1)
    nf = pl.num_programs(1)
    x = x_ref[0]
    h1 = _dot(x, w1_ref[0].astype(BF16))
    h3 = _dot(x, w3_ref[0].astype(BF16))
    he = (_silu(h1) * h3).astype(BF16)
    contrib = _dot(he, w2_ref[0].astype(BF16))

    @pl.when(f == 0)
    def _():
        acc_scr[...] = contrib

    @pl.when(f > 0)
    def _():
        acc_scr[...] = acc_scr[...] + contrib

    @pl.when(f == nf - 1)
    def _():
        o_ref[0] = acc_scr[...].astype(o_ref.dtype)


def _ffn(xe, w1, w3, w2):
    cap = xe.shape[1]
    fc = 256
    return pl.pallas_call(
        _ffn_kernel,
        grid=(N_EXPERTS, EXPERT_FF // fc),
        in_specs=[
            pl.BlockSpec((1, cap, D), lambda e, f: (e, 0, 0)),
            pl.BlockSpec((1, D, fc), lambda e, f: (e, 0, f)),
            pl.BlockSpec((1, D, fc), lambda e, f: (e, 0, f)),
            pl.BlockSpec((1, fc, D), lambda e, f: (e, f, 0)),
        ],
        out_specs=pl.BlockSpec((1, cap, D), lambda e, f: (e, 0, 0)),
        out_shape=jax.ShapeDtypeStruct((N_EXPERTS, cap, D), BF16),
        scratch_shapes=[pltpu.VMEM((cap, D), F32)],
        compiler_params=_params(("arbitrary", "arbitrary")),
        name="ffn",
    )(xe, w1, w3, w2)


def _combine_kernel(j0_ref, pos_ref, gate_ref, ya_ref, yb_ref, x_ref, m5_ref, fw_ref, o_ref, acc_scr, *, sb, final):
    b = pl.program_id(0)
    e = pl.program_id(1)
    j0 = j0_ref[b * N_EXPERTS + e]
    tb = pos_ref.shape[0]
    lane = lax.broadcasted_iota(I32, (tb, LANES), 1)
    mine = lane == e
    posc = jnp.sum(jnp.where(mine, pos_ref[...], 0.0), axis=1, keepdims=True)
    gc = jnp.sum(jnp.where(mine, gate_ref[...], 0.0), axis=1, keepdims=True)
    rel = posc - (j0 * sb).astype(F32)
    slot = lax.broadcasted_iota(I32, (tb, sb), 1).astype(F32)
    s_a = jnp.where(rel == slot, 1.0, 0.0).astype(BF16)
    s_b = jnp.where(rel - sb == slot, 1.0, 0.0).astype(BF16)
    contrib = (_dot(s_a, ya_ref[0]) + _dot(s_b, yb_ref[0])) * gc

    @pl.when(e == 0)
    def _():
        acc_scr[...] = contrib

    @pl.when(e > 0)
    def _():
        acc_scr[...] = acc_scr[...] + contrib

    @pl.when(e == N_EXPERTS - 1)
    def _():
        y = x_ref[...] + m5_ref[...] * acc_scr[...]
        if final:
            y = y * lax.rsqrt(jnp.mean(y * y, axis=-1, keepdims=True) + EPS) * fw_ref[...]
        o_ref[...] = y


def _combine(j0, post, aff, ye, x, m5, fw, sb, tb, final):
    n = x.shape[0]
    cap = ye.shape[1]
    nj = cap // sb
    vec = pl.BlockSpec((1, D), lambda b, e, j0: (0, 0))
    grid_spec = pltpu.PrefetchScalarGridSpec(
        num_scalar_prefetch=1,
        grid=(n // tb, N_EXPERTS),
        in_specs=[
            pl.BlockSpec((tb, LANES), lambda b, e, j0: (b, 0)),
            pl.BlockSpec((tb, LANES), lambda b, e, j0: (b, 0)),
            pl.BlockSpec((1, sb, D), lambda b, e, j0: (e, j0[b * N_EXPERTS + e], 0)),
            pl.BlockSpec((1, sb, D), lambda b, e, j0: (e, jnp.minimum(j0[b * N_EXPERTS + e] + 1, nj - 1), 0)),
            pl.BlockSpec((tb, D), lambda b, e, j0: (b, 0)),
            vec, vec,
        ],
        out_specs=pl.BlockSpec((tb, D), lambda b, e, j0: (b, 0)),
        scratch_shapes=[pltpu.VMEM((tb, D), F32)],
    )
    return pl.pallas_call(
        functools.partial(_combine_kernel, sb=sb, final=final),
        grid_spec=grid_spec,
        out_shape=jax.ShapeDtypeStruct((n, D), F32),
        compiler_params=_params(("arbitrary", "arbitrary")),
        name="combine",
    )(j0, post, aff, ye, ye, x, m5, fw)


def _pair_tables(off_tb, cap, sb, kp):
    ne, nt = off_tb.shape
    nxt = jnp.concatenate([off_tb[:, 1:], jnp.full((ne, 1), cap, I32)], axis=1)
    cnt = nxt - off_tb
    j_lo = off_tb // sb
    j_hi = jnp.where(cnt > 0, (nxt - 1) // sb, j_lo - 1)
    npair = j_hi - j_lo + 1
    starts = jnp.cumsum(npair, axis=1) - npair
    total = jnp.sum(npair, axis=1, keepdims=True)
    k = jnp.arange(kp, dtype=I32)[None, :]
    kk = jnp.minimum(k, total - 1)
    tb_k = jnp.sum((starts[:, None, :] <= kk[:, :, None]).astype(I32), axis=2) - 1
    j_k = jnp.take_along_axis(j_lo, tb_k, axis=1) + kk - jnp.take_along_axis(starts, tb_k, axis=1)
    valid = (k < total).astype(I32)
    prev_j = jnp.concatenate([jnp.full((ne, 1), -1, I32), j_k[:, :-1]], axis=1)
    first = ((j_k != prev_j) & (k < total)).astype(I32)
    flags = valid + 2 * first
    return j_k.reshape(-1).astype(I32), tb_k.reshape(-1).astype(I32), flags.reshape(-1).astype(I32)


def _moe(x, nw, sh, sc, m5, rw, w1, w3, w2, fw, final):
    n = x.shape[0]
    cap = CAPACITY_FACTOR * n // N_EXPERTS
    sb = min(256, cap)
    tb = min(256, n)
    nj = cap // sb
    nt = n // tb
    assert nj == 1 or tb <= sb
    kp = nj + nt
    h2, aff, afft = _router(x, nw, sh, sc, rw)
    pose, post, off = _select(afft, cap)
    off128 = off[:, :, 0].T.astype(I32)
    off_tb = off128[:, :: tb // LANES]
    jt, tt, fl = _pair_tables(off_tb, cap, sb, kp)
    xe = _gather(jt, tt, fl, pose, h2, cap, sb, tb, kp)
    ye = _ffn(xe, w1, w3, w2)
    j0 = (off_tb // sb).T.reshape(-1).astype(I32)
    return _combine(j0, post, aff, ye, x, m5, fw, sb, tb, final)


def _rope_tables(n_rows):
    row = jnp.repeat(jnp.arange(n_rows, dtype=F32), GRID_W)
    col = jnp.tile(jnp.arange(GRID_W, dtype=F32), n_rows)
    half = DIFF_HEAD_DIM // 2
    freqs = ROPE_BASE ** (-jnp.arange(0, half, 2, dtype=F32) / half)
    ar, ac = row[:, None] * freqs, col[:, None] * freqs
    cr, sr, cc, sn = jnp.cos(ar), jnp.sin(ar), jnp.cos(ac), jnp.sin(ac)
    z = jnp.zeros_like(sr)
    cos = jnp.concatenate([cr, cr, cc, cc], axis=1)
    sa = jnp.concatenate([-sr, z, -sn, z], axis=1)
    sb = jnp.concatenate([z, sr, z, sn], axis=1)
    return tuple(jnp.tile(t, (1, 2)) for t in (cos, sa, sb))


def _identity_tables(n):
    return (jnp.ones((n, LANES), F32), jnp.zeros((n, LANES), F32), jnp.zeros((n, LANES), F32))


def _proj_weight(w_in):
    sizes = (1024, 256, 32, 1024, 1024, 256, 1024, 1024, 2048, 3072)
    idx = np.cumsum((0,) + sizes)
    px, pb, pdt, pk, pv, pc, pq, pz, pglu, pgate = [w_in[:, idx[i]:idx[i + 1]] for i in range(10)]
    pad = jnp.zeros((D, LANES - SSD_HEADS), w_in.dtype)
    tail = jnp.zeros((D, N_PROJ - OFF_DT - 2 * LANES), w_in.dtype)
    w = jnp.concatenate([px, pz, pk, pv, pq, pglu, pgate, pb, pc,
                         pdt[:, :SSD_HEADS], pad, pdt[:, SSD_HEADS:], pad, tail], axis=1)
    return w.astype(BF16)


def _lane_pad(v):
    return jnp.pad(v, ((0, 0), (0, LANES - v.shape[1]))).reshape(2, 1, LANES)


def _mixer_front(x, nw, sh, sc, w, conv_w, conv_b):
    p = _inproj(x, nw, sh, sc, w)
    xbc = _ssdconv(p, conv_w, conv_b)
    return p, xbc


def kernel(x, c, ctx, c_ctx, ada_w, ada_b, norm1_w, norm2_w, w_in, ssd_conv_w, ssd_conv_b, ssd_dt_bias, ssd_a_log, ssd_d, ssd_norm_w, ssd_out, diff_lambda, diff_subln_w, diff_out, conf_dw_w, conf_dw_b, conf_ln_w, conf_ln_b, conf_out, w_o, router_w, exp_w1, exp_w3, exp_w2, final_norm_w):
    depth = ada_w.shape[0]
    n = x.shape[1]
    nctx = ctx.shape[1]
    x_lat = x[0]
    x_ctx = ctx[0]

    cc = jnp.zeros((8, D), F32).at[0].set(c[0]).at[1].set(c_ctx)
    mods = _ada(cc, ada_w, ada_b)

    rope = _rope_tables(n // GRID_W)
    ident_ctx = _identity_tables(nctx)
    scale = DIFF_HEAD_DIM ** -0.5
    zero_state = jnp.zeros((2, SSD_STATE, SSD_HEADS * SSD_HEAD_DIM), F32)
    fw = final_norm_w.reshape(1, D)

    for l in range(depth):
        last = l == depth - 1
        lam_init = 0.8 - 0.6 * math.exp(-0.3 * l)
        lq1, lk1, lq2, lk2 = diff_lambda[l].astype(F32)
        lam = (jnp.exp(jnp.sum(lq1 * lk1)) - jnp.exp(jnp.sum(lq2 * lk2)) + lam_init).reshape(1)
        m_lat = [mods[l, 0:1, i * D:(i + 1) * D] for i in range(N_MOD)]
        m_ctx = [mods[l, 1:2, i * D:(i + 1) * D] for i in range(N_MOD)]
        nw1 = norm1_w[l].reshape(1, D)
        nw2 = norm2_w[l].reshape(1, D)
        w = _proj_weight(w_in[l])
        dtb = _lane_pad(ssd_dt_bias[l])
        alog = _lane_pad(ssd_a_log[l])
        dsk = jnp.repeat(ssd_d[l], SSD_HEAD_DIM).reshape(1, D)
        gnw = ssd_norm_w[l].reshape(1, D)
        sw = diff_subln_w[l].reshape(1, LANES)
        wa, wb, wc, wo = (t[l].astype(BF16) for t in (ssd_out, diff_out, conf_out, w_o))
        rw = jnp.pad(router_w[l], ((0, 0), (0, LANES - N_EXPERTS))).astype(BF16)
        coef = 1.0 - lam_init

        p_c, xbc_c = _mixer_front(x_ctx, nw1, m_ctx[0], m_ctx[1], w, ssd_conv_w[l], ssd_conv_b[l])
        y2_c, st_c = _ssd(xbc_c, p_c, dtb, alog, zero_state)
        p_l, xbc_l = _mixer_front(x_lat, nw1, m_lat[0], m_lat[1], w, ssd_conv_w[l], ssd_conv_b[l])
        k_all, v_all = _kvprep(p_c, p_l, rope)

        if not last:
            q_c = _qprep(p_c, ident_ctx, scale)
            on_c = _attention(lam, q_c, k_all, v_all, sw, nctx, coef)
            cn_c = _conformer(p_c, conf_dw_w[l], conf_dw_b[l], conf_ln_w[l], conf_ln_b[l])
            x_ctx = _merge(y2_c, xbc_c, p_c, on_c, cn_c, x_ctx, dsk, gnw, m_ctx[2], wa, wb, wc, wo)
            x_ctx = _moe(x_ctx, nw2, m_ctx[3], m_ctx[4], m_ctx[5], rw, exp_w1[l], exp_w3[l], exp_w2[l], fw, False)

        y2_l, _ = _ssd(xbc_l, p_l, dtb, alog, st_c)
        q_l = _qprep(p_l, rope, scale)
        on_l = _attention(lam, q_l, k_all, v_all, sw, nctx + n, coef)
        cn_l = _conformer(p_l, conf_dw_w[l], conf_dw_b[l], conf_ln_w[l], conf_ln_b[l])
        x_lat = _merge(y2_l, xbc_l, p_l, on_l, cn_l, x_lat, dsk, gnw, m_lat[2], wa, wb, wc, wo)
        x_lat = _moe(x_lat, nw2, m_lat[3], m_lat[4], m_lat[5], rw, exp_w1[l], exp_w3[l], exp_w2[l], fw, last)

    return x_lat[None]
```

```python
import functools
import math

import jax
import jax.numpy as jnp
import numpy as np
from jax import lax
from jax.experimental import pallas as pl
from jax.experimental.pallas import tpu as pltpu

F32 = jnp.float32
BF16 = jnp.bfloat16
I32 = jnp.int32

EPS = 1e-6
D = 1024
N_MOD = 6
GRID_W = 64
SSD_HEADS = 16
SSD_HEAD_DIM = 64
SSD_STATE = 128
SSD_GN = 256
SSD_CHUNK = 128
DIFF_HEADS = 8
DIFF_HEAD_DIM = 64
ROPE_BASE = 10000.0
CONF_KERNEL = 31
CONF_HALO = 16
N_EXPERTS = 16
EXPERT_FF = 2048
CAPACITY_FACTOR = 2
LANES = 128

OFF_X, OFF_Z, OFF_K, OFF_V, OFF_Q = 0, 1024, 2048, 3072, 4096
OFF_GLU_A, OFF_GLU_G, OFF_GATE = 5120, 6144, 7168
OFF_B, OFF_C, OFF_DT = 10240, 10496, 10752
N_PROJ = 11264

VMEM_LIMIT = 56 * 1024 * 1024


def _params(sem, vmem=VMEM_LIMIT):
    return pltpu.CompilerParams(dimension_semantics=sem, vmem_limit_bytes=vmem)


def _silu(x):
    return x * jax.nn.sigmoid(x)


def _split3(x):
    a1 = x.astype(BF16)
    r1 = x - a1.astype(F32)
    a2 = r1.astype(BF16)
    a3 = (r1 - a2.astype(F32)).astype(BF16)
    return a1, a2, a3


def _dot(a, b):
    return jnp.dot(a, b, preferred_element_type=F32)


def _ada_kernel(c_ref, w_ref, b_ref, o_ref):
    a = _silu(c_ref[...]).astype(BF16)
    o_ref[0] = _dot(a, w_ref[0].astype(BF16)) + b_ref[0]


def _ada(cc, ada_w, ada_b):
    nl = ada_w.shape[0]
    tn = 1536
    return pl.pallas_call(
        _ada_kernel,
        grid=(nl, N_MOD * D // tn),
        in_specs=[
            pl.BlockSpec((8, D), lambda l, j: (0, 0)),
            pl.BlockSpec((1, D, tn), lambda l, j: (l, 0, j)),
            pl.BlockSpec((1, 1, tn), lambda l, j: (l, 0, j)),
        ],
        out_specs=pl.BlockSpec((1, 8, tn), lambda l, j: (l, 0, j)),
        out_shape=jax.ShapeDtypeStruct((nl, 8, N_MOD * D), F32),
        compiler_params=_params(("arbitrary", "arbitrary")),
        name="ada",
    )(cc, ada_w, ada_b.reshape(nl, 1, N_MOD * D))


def _modulate(x, nw, sh, sc):
    r = lax.rsqrt(jnp.mean(x * x, axis=-1, keepdims=True) + EPS)
    return (x * r * nw) * (1.0 + sc) + sh


def _inproj_kernel(x_ref, nw_ref, sh_ref, sc_ref, w_ref, o_ref, h_scr):
    @pl.when(pl.program_id(1) == 0)
    def _():
        h_scr[...] = _modulate(x_ref[...], nw_ref[...], sh_ref[...], sc_ref[...]).astype(BF16)

    o_ref[...] = _dot(h_scr[...], w_ref[...])


def _inproj(x, nw, sh, sc, w):
    n = x.shape[0]
    tm = min(n, 1024)
    tn = 1024
    vec = pl.BlockSpec((1, D), lambda i, j: (0, 0))
    return pl.pallas_call(
        _inproj_kernel,
        grid=(n // tm, N_PROJ // tn),
        in_specs=[pl.BlockSpec((tm, D), lambda i, j: (i, 0)), vec, vec, vec,
                  pl.BlockSpec((D, tn), lambda i, j: (0, j))],
        out_specs=pl.BlockSpec((tm, tn), lambda i, j: (i, j)),
        out_shape=jax.ShapeDtypeStruct((n, N_PROJ), F32),
        scratch_shapes=[pltpu.VMEM((tm, D), BF16)],
        compiler_params=_params(("arbitrary", "arbitrary")),
        name="inproj",
    )(x, nw, sh, sc, w)


def _ssdconv_kernel(prev_ref, cur_ref, next_ref, w_ref, b_ref, o_ref):
    i = pl.program_id(0)
    nt = pl.num_programs(0)
    u = cur_ref[...]
    t = u.shape[0]
    row = lax.broadcasted_iota(I32, u.shape, 0)
    before = jnp.where(i > 0, prev_ref[7:8, :], 0.0)
    after = jnp.where(i < nt - 1, next_ref[0:1, :], 0.0)
    um1 = jnp.where(row == 0, before, pltpu.roll(u, 1, 0))
    up1 = jnp.where(row == t - 1, after, pltpu.roll(u, t - 1, 0))
    y = um1 * w_ref[0:1, :] + u * w_ref[1:2, :] + up1 * w_ref[2:3, :] + b_ref[...]
    o_ref[...] = _silu(y)


def _ssdconv(p, conv_w, conv_b):
    n = p.shape[0]
    t = min(n, 512)
    cw = 512
    r8 = t // 8
    nb8 = n // 8

    def col(c):
        return jnp.where(c < 2, c, OFF_B // cw)

    return pl.pallas_call(
        _ssdconv_kernel,
        grid=(n // t, 3),
        in_specs=[
            pl.BlockSpec((8, cw), lambda i, c: (jnp.maximum(i * r8 - 1, 0), col(c))),
            pl.BlockSpec((t, cw), lambda i, c: (i, col(c))),
            pl.BlockSpec((8, cw), lambda i, c: (jnp.minimum((i + 1) * r8, nb8 - 1), col(c))),
            pl.BlockSpec((3, cw), lambda i, c: (0, c)),
            pl.BlockSpec((1, cw), lambda i, c: (0, c)),
        ],
        out_specs=pl.BlockSpec((t, cw), lambda i, c: (i, c)),
        out_shape=jax.ShapeDtypeStruct((n, 3 * cw), F32),
        compiler_params=_params(("arbitrary", "arbitrary")),
        name="ssdconv",
    )(p, p, p, conv_w, conv_b.reshape(1, -1))


def _ssd_kernel(xs_ref, b_ref, c_ref, dt_ref, dtb_ref, alog_ref, h0_ref, y_ref, hT_ref, st_scr):
    d = pl.program_id(0)
    c = pl.program_id(1)
    nc = pl.num_programs(1)
    L = SSD_CHUNK
    P = SSD_HEAD_DIM
    HP = SSD_HEADS * P
    GW = HP // 2

    @pl.when(c == 0)
    def _():
        st_scr[...] = h0_ref[0]

    fwd = d == 0
    x = dt_ref[...] + dtb_ref[0]
    dt = jnp.maximum(x, 0.0) + jnp.log1p(jnp.exp(-jnp.abs(x)))
    a = dt * (-jnp.exp(alog_ref[0]))

    li = lax.broadcasted_iota(I32, (L, L), 0)
    si = lax.broadcasted_iota(I32, (L, L), 1)
    keep = (li - si) * jnp.where(fwd, 1, -1) >= 0
    tri = jnp.where(keep, 1.0, 0.0).astype(BF16)
    a1, a2, a3 = _split3(a)
    cum = _dot(tri, a1) + _dot(tri, a2) + _dot(tri, a3)
    cum_t = cum.T

    ex = jnp.where(lax.broadcasted_iota(I32, (LANES, HP), 0) == lax.broadcasted_iota(I32, (LANES, HP), 1) // P,
                   1.0, 0.0).astype(BF16)

    def expand(v):
        v1, v2, v3 = _split3(v)
        return _dot(v1, ex) + _dot(v2, ex) + _dot(v3, ex)

    cum_e = expand(cum)
    dt_e = expand(dt)
    tot_e = jnp.where(fwd, cum_e[L - 1:L, :], cum_e[0:1, :])

    xdt = xs_ref[...] * dt_e
    xb = xdt.astype(BF16)
    xw = (xdt * jnp.exp(tot_e - cum_e)).astype(BF16)
    bb = b_ref[...].astype(BF16)
    cb = c_ref[...].astype(BF16)
    st = st_scr[...]
    stb = st.astype(BF16)
    grow = jnp.exp(cum_e)

    for g in range(2):
        bg = bb[:, g * SSD_STATE:(g + 1) * SSD_STATE]
        cg = cb[:, g * SSD_STATE:(g + 1) * SSD_STATE]
        scores = lax.dot_general(cg, bg, (((1,), (1,)), ((), ())), preferred_element_type=F32)
        y_off = _dot(cg, stb[:, g * GW:(g + 1) * GW]) * grow[:, g * GW:(g + 1) * GW]
        for hh in range(SSD_HEADS // 2):
            h = g * (SSD_HEADS // 2) + hh
            seg = cum[:, h:h + 1] - cum_t[h:h + 1, :]
            dec = jnp.where(keep, jnp.exp(jnp.where(keep, seg, 0.0)), 0.0)
            m = (scores * dec).astype(BF16)
            yd = _dot(m, xb[:, h * P:(h + 1) * P])
            y_ref[0, :, h * P:(h + 1) * P] = yd + y_off[:, hh * P:(hh + 1) * P]
        upd = lax.dot_general(bg, xw[:, g * GW:(g + 1) * GW], (((0,), (0,)), ((), ())),
                              preferred_element_type=F32)
        st_scr[:, g * GW:(g + 1) * GW] = st[:, g * GW:(g + 1) * GW] * jnp.exp(tot_e[:, g * GW:(g + 1) * GW]) + upd

    @pl.when(c == nc - 1)
    def _():
        hT_ref[0] = st_scr[...]


def _ssd(xbc, p, dtb, alog, h0):
    n = xbc.shape[0]
    L = SSD_CHUNK
    nc = n // L
    HP = SSD_HEADS * SSD_HEAD_DIM

    def chunk(d, c):
        return c + d * (nc - 1 - 2 * c)

    return pl.pallas_call(
        _ssd_kernel,
        grid=(2, nc),
        in_specs=[
            pl.BlockSpec((L, HP), lambda d, c: (chunk(d, c), 0)),
            pl.BlockSpec((L, SSD_GN), lambda d, c: (chunk(d, c), HP // SSD_GN)),
            pl.BlockSpec((L, SSD_GN), lambda d, c: (chunk(d, c), HP // SSD_GN + 1)),
            pl.BlockSpec((L, LANES), lambda d, c: (chunk(d, c), OFF_DT // LANES + d)),
            pl.BlockSpec((1, 1, LANES), lambda d, c: (d, 0, 0)),
            pl.BlockSpec((1, 1, LANES), lambda d, c: (d, 0, 0)),
            pl.BlockSpec((1, SSD_STATE, HP), lambda d, c: (d, 0, 0)),
        ],
        out_specs=[
            pl.BlockSpec((1, L, HP), lambda d, c: (d, chunk(d, c), 0)),
            pl.BlockSpec((1, SSD_STATE, HP), lambda d, c: (d, 0, 0)),
        ],
        out_shape=[jax.ShapeDtypeStruct((2, n, HP), F32), jax.ShapeDtypeStruct((2, SSD_STATE, HP), F32)],
        scratch_shapes=[pltpu.VMEM((SSD_STATE, HP), F32)],
        compiler_params=_params(("arbitrary", "arbitrary")),
        name="ssd",
    )(xbc, xbc, xbc, p, dtb, alog, h0)


def _rope_tile(x_ref, o_ref, cos, sa, sb, scale):
    for j in range(D // LANES):
        xj = x_ref[:, j * LANES:(j + 1) * LANES]
        yj = xj * cos + pltpu.roll(xj, LANES - 16, 1) * sa + pltpu.roll(xj, 16, 1) * sb
        o_ref[:, j * LANES:(j + 1) * LANES] = (yj * scale).astype(o_ref.dtype)


def _qprep_kernel(q_ref, cos_ref, sa_ref, sb_ref, o_ref, *, scale):
    _rope_tile(q_ref, o_ref, cos_ref[...], sa_ref[...], sb_ref[...], scale)


def _qprep(p, tabs, scale):
    n = p.shape[0]
    t = min(n, 256)
    tab = pl.BlockSpec((t, LANES), lambda i: (i, 0))
    return pl.pallas_call(
        functools.partial(_qprep_kernel, scale=scale),
        grid=(n // t,),
        in_specs=[pl.BlockSpec((t, D), lambda i: (i, OFF_Q // D)), tab, tab, tab],
        out_specs=pl.BlockSpec((t, D), lambda i: (i, 0)),
        out_shape=jax.ShapeDtypeStruct((n, D), BF16),
        compiler_params=_params(("arbitrary",)),
        name="qprep",
    )(p, *tabs)


def _kvprep_kernel(kc_ref, vc_ref, kl_ref, vl_ref, cos_ref, sa_ref, sb_ref, ko_ref, vo_ref):
    i = pl.program_id(0)

    @pl.when(i == 0)
    def _():
        ko_ref[...] = kc_ref[...].astype(BF16)
        vo_ref[...] = vc_ref[...].astype(BF16)

    @pl.when(i > 0)
    def _():
        _rope_tile(kl_ref, ko_ref, cos_ref[...], sa_ref[...], sb_ref[...], 1.0)
        vo_ref[...] = vl_ref[...].astype(BF16)


def _kvprep(p_ctx, p_lat, tabs):
    nctx = p_ctx.shape[0]
    n = p_lat.shape[0]
    t = nctx
    m = nctx + n

    def lat(i):
        return jnp.maximum(i - 1, 0)

    tab = pl.BlockSpec((t, LANES), lambda i: (lat(i), 0))
    out = pl.BlockSpec((t, D), lambda i: (i, 0))
    return pl.pallas_call(
        _kvprep_kernel,
        grid=(m // t,),
        in_specs=[
            pl.BlockSpec((t, D), lambda i: (0, OFF_K // D)),
            pl.BlockSpec((t, D), lambda i: (0, OFF_V // D)),
            pl.BlockSpec((t, D), lambda i: (lat(i), OFF_K // D)),
            pl.BlockSpec((t, D), lambda i: (lat(i), OFF_V // D)),
            tab, tab, tab,
        ],
        out_specs=[out, out],
        out_shape=[jax.ShapeDtypeStruct((m, D), BF16), jax.ShapeDtypeStruct((m, D), BF16)],
        compiler_params=_params(("arbitrary",)),
        name="kvprep",
    )(p_ctx, p_ctx, p_lat, p_lat, *tabs)


def _attn_kernel(lam_ref, q_ref, k_ref, v_ref, sw_ref, o_ref, q2_scr, sa_scr, sb_scr, p_scr, m_scr, acc_scr,
                 *, tq, ck, nchunks, coef):
    q = q_ref[...]
    lane = lax.broadcasted_iota(I32, q.shape, 1)
    zero = jnp.zeros_like(q)
    q2_scr[0:tq, :] = jnp.where(lane < DIFF_HEAD_DIM, q, zero)
    q2_scr[tq:2 * tq, :] = jnp.where(lane >= DIFF_HEAD_DIM, q, zero)
    m_scr[...] = jnp.full(m_scr.shape, -jnp.inf, F32)
    acc_scr[...] = jnp.zeros(acc_scr.shape, F32)
    ones = jnp.ones((ck, LANES), BF16)
    nb = ck // LANES

    def rows(c):
        return pl.ds(c * ck, ck) if isinstance(c, int) else pl.ds(pl.multiple_of(c * ck, ck), ck)

    def scores(c, s_scr):
        s_scr[...] = lax.dot_general(q2_scr[...], k_ref[rows(c), :], (((1,), (1,)), ((), ())),
                                     preferred_element_type=F32)

    def soft_pv(c, s_scr):
        mx = s_scr[:, 0:LANES]
        for b in range(1, nb):
            mx = jnp.maximum(mx, s_scr[:, b * LANES:(b + 1) * LANES])
        m_prev = m_scr[...]
        m_new = jnp.maximum(m_prev, jnp.max(mx, axis=1, keepdims=True))
        alpha = jnp.exp2(m_prev - m_new)
        for b in range(nb):
            p_scr[:, b * LANES:(b + 1) * LANES] = jnp.exp2(s_scr[:, b * LANES:(b + 1) * LANES] - m_new).astype(BF16)
        va = jnp.concatenate([v_ref[rows(c), :], ones], axis=1)
        acc_scr[...] = acc_scr[...] * jnp.concatenate([alpha, alpha], axis=1) + _dot(p_scr[...], va)
        m_scr[...] = m_new

    scores(0, sa_scr)

    def pair(jj, carry):
        scores(2 * jj + 1, sb_scr)
        soft_pv(2 * jj, sa_scr)
        scores(2 * jj + 2, sa_scr)
        soft_pv(2 * jj + 1, sb_scr)
        return carry

    lax.fori_loop(0, (nchunks - 1) // 2, pair, 0)
    if (nchunks - 1) % 2 == 1:
        scores(nchunks - 1, sb_scr)
        soft_pv(nchunks - 2, sa_scr)
        soft_pv(nchunks - 1, sb_scr)
    else:
        soft_pv(nchunks - 1, sa_scr)

    on = acc_scr[:, 0:LANES] / acc_scr[:, LANES:2 * LANES]
    o = on[0:tq, :] - lam_ref[0] * on[tq:2 * tq, :]
    r = lax.rsqrt(jnp.mean(o * o, axis=-1, keepdims=True) + EPS)
    o_ref[...] = ((o * r * sw_ref[...]) * coef).astype(o_ref.dtype)


def _pick_tk(m):
    for cand in (1280, 1024, 512, 256):
        if m % cand == 0:
            return cand
    return m


def _attention(lam, q, k, v, subln_w, m, coef):
    n = q.shape[0]
    tq = min(n, 256)
    ck = _pick_tk(m)
    return pl.pallas_call(
        functools.partial(_attn_kernel, tq=tq, ck=ck, nchunks=m // ck, coef=coef),
        grid=(DIFF_HEADS, n // tq),
        in_specs=[
            pl.BlockSpec(memory_space=pltpu.SMEM),
            pl.BlockSpec((tq, LANES), lambda h, i: (i, h)),
            pl.BlockSpec((m, LANES), lambda h, i: (0, h)),
            pl.BlockSpec((m, LANES), lambda h, i: (0, h)),
            pl.BlockSpec((1, LANES), lambda h, i: (0, 0)),
        ],
        out_specs=pl.BlockSpec((tq, LANES), lambda h, i: (i, h)),
        out_shape=jax.ShapeDtypeStruct((n, D), BF16),
        scratch_shapes=[pltpu.VMEM((2 * tq, LANES), BF16), pltpu.VMEM((2 * tq, ck), F32),
                        pltpu.VMEM((2 * tq, ck), F32), pltpu.VMEM((2 * tq, ck), BF16),
                        pltpu.VMEM((2 * tq, LANES), F32), pltpu.VMEM((2 * tq, 2 * LANES), F32)],
        compiler_params=_params(("arbitrary", "arbitrary")),
        name="attn",
    )(lam, q, k, v, subln_w)


def _conf_kernel(ap_ref, gp_ref, a_ref, g_ref, an_ref, gn_ref, w_ref, b_ref, lw_ref, lb_ref, o_ref, ext_scr):
    i = pl.program_id(0)
    nt = pl.num_programs(0)
    t = a_ref.shape[0]
    hl = CONF_HALO

    def glu(a, g):
        return a[...] * jax.nn.sigmoid(g[...])

    ext_scr[0:hl, :] = jnp.where(i > 0, glu(ap_ref, gp_ref), 0.0)
    ext_scr[hl:hl + t, :] = glu(a_ref, g_ref)
    ext_scr[hl + t:2 * hl + t, :] = jnp.where(i < nt - 1, glu(an_ref, gn_ref), 0.0)

    acc = jnp.zeros((t, D), F32) + b_ref[...]
    for k in range(CONF_KERNEL):
        acc = acc + ext_scr[pl.ds(k + hl - CONF_KERNEL // 2, t), :] * w_ref[k:k + 1, :]
    mu = jnp.mean(acc, axis=-1, keepdims=True)
    xc = acc - mu
    y = xc * lax.rsqrt(jnp.mean(xc * xc, axis=-1, keepdims=True) + EPS) * lw_ref[...] + lb_ref[...]
    o_ref[...] = _silu(y).astype(o_ref.dtype)


def _conformer(p, dw_w, dw_b, ln_w, ln_b):
    n = p.shape[0]
    t = min(n, 256)
    hl = CONF_HALO
    rh = t // hl
    nbh = n // hl
    ca, cg = OFF_GLU_A // D, OFF_GLU_G // D

    def prev(i):
        return jnp.maximum(i * rh - 1, 0)

    def nxt(i):
        return jnp.minimum((i + 1) * rh, nbh - 1)

    vec = pl.BlockSpec((1, D), lambda i: (0, 0))
    return pl.pallas_call(
        _conf_kernel,
        grid=(n // t,),
        in_specs=[
            pl.BlockSpec((hl, D), lambda i: (prev(i), ca)), pl.BlockSpec((hl, D), lambda i: (prev(i), cg)),
            pl.BlockSpec((t, D), lambda i: (i, ca)), pl.BlockSpec((t, D), lambda i: (i, cg)),
            pl.BlockSpec((hl, D), lambda i: (nxt(i), ca)), pl.BlockSpec((hl, D), lambda i: (nxt(i), cg)),
            pl.BlockSpec((CONF_KERNEL, D), lambda i: (0, 0)), vec, vec, vec,
        ],
        out_specs=pl.BlockSpec((t, D), lambda i: (i, 0)),
        out_shape=jax.ShapeDtypeStruct((n, D), BF16),
        scratch_shapes=[pltpu.VMEM((t + 2 * hl, D), F32)],
        compiler_params=_params(("arbitrary",)),
        name="conformer",
    )(p, p, p, p, p, p, dw_w, dw_b.reshape(1, D), ln_w.reshape(1, D), ln_b.reshape(1, D))


def _merge_kernel(yf_ref, yb_ref, xs_ref, z_ref, on_ref, cn_ref, g0_ref, g1_ref, g2_ref, x_ref, dsk_ref, nw_ref,
                  m2_ref, wa_ref, wb_ref, wc_ref, wo_ref, o_ref):
    y = yf_ref[0] + yb_ref[0] + xs_ref[...] * dsk_ref[...]
    yz = y * _silu(z_ref[...])
    half = D // 2
    parts = []
    for g in range(2):
        seg = yz[:, g * half:(g + 1) * half]
        r = lax.rsqrt(jnp.mean(seg * seg, axis=-1, keepdims=True) + EPS)
        parts.append(seg * r * nw_ref[:, g * half:(g + 1) * half])
    gn = jnp.concatenate(parts, axis=1).astype(BF16)
    y_a = _dot(gn, wa_ref[...])
    y_b = _dot(on_ref[...], wb_ref[...])
    y_c = _dot(cn_ref[...], wc_ref[...])
    m = jax.nn.sigmoid(g0_ref[...]) * y_a + jax.nn.sigmoid(g1_ref[...]) * y_b + jax.nn.sigmoid(g2_ref[...]) * y_c
    y_o = _dot(m.astype(BF16), wo_ref[...])
    o_ref[...] = x_ref[...] + m2_ref[...] * y_o


def _merge(y2, xbc, p, on, cn, x, dsk, nw, m2, wa, wb, wc, wo):
    n = x.shape[0]
    tm = min(n, 256)
    vec = pl.BlockSpec((1, D), lambda i: (0, 0))
    wsp = pl.BlockSpec((D, D), lambda i: (0, 0))
    row = pl.BlockSpec((tm, D), lambda i: (i, 0))
    return pl.pallas_call(
        _merge_kernel,
        grid=(n // tm,),
        in_specs=[
            pl.BlockSpec((1, tm, D), lambda i: (0, i, 0)),
            pl.BlockSpec((1, tm, D), lambda i: (1, i, 0)),
            row,
            pl.BlockSpec((tm, D), lambda i: (i, OFF_Z // D)),
            row, row,
            pl.BlockSpec((tm, D), lambda i: (i, OFF_GATE // D)),
            pl.BlockSpec((tm, D), lambda i: (i, OFF_GATE // D + 1)),
            pl.BlockSpec((tm, D), lambda i: (i, OFF_GATE // D + 2)),
            row, vec, vec, vec, wsp, wsp, wsp, wsp,
        ],
        out_specs=row,
        out_shape=jax.ShapeDtypeStruct((n, D), F32),
        compiler_params=_params(("arbitrary",)),
        name="merge",
    )(y2, y2, xbc, p, on, cn, p, p, p, x, dsk, nw, m2, wa, wb, wc, wo)


def _router_kernel(x_ref, nw_ref, sh_ref, sc_ref, rw_ref, h_ref, aff_ref, afft_ref):
    h = _modulate(x_ref[...], nw_ref[...], sh_ref[...], sc_ref[...]).astype(BF16)
    h_ref[...] = h
    logits = _dot(h, rw_ref[...])
    lane = lax.broadcasted_iota(I32, logits.shape, 1)
    logits = jnp.where(lane < N_EXPERTS, logits, -jnp.inf)
    e = jnp.exp(logits - jnp.max(logits, axis=-1, keepdims=True))
    aff = e / jnp.sum(e, axis=-1, keepdims=True)
    aff_ref[...] = aff
    afft_ref[...] = aff.T[0:N_EXPERTS, :]


def _router(x, nw, sh, sc, rw):
    n = x.shape[0]
    tm = min(n, 512)
    vec = pl.BlockSpec((1, D), lambda i: (0, 0))
    return pl.pallas_call(
        _router_kernel,
        grid=(n // tm,),
        in_specs=[pl.BlockSpec((tm, D), lambda i: (i, 0)), vec, vec, vec, pl.BlockSpec((D, LANES), lambda i: (0, 0))],
        out_specs=[pl.BlockSpec((tm, D), lambda i: (i, 0)), pl.BlockSpec((tm, LANES), lambda i: (i, 0)),
                   pl.BlockSpec((N_EXPERTS, tm), lambda i: (0, i))],
        out_shape=[jax.ShapeDtypeStruct((n, D), BF16), jax.ShapeDtypeStruct((n, LANES), F32),
                   jax.ShapeDtypeStruct((N_EXPERTS, n), F32)],
        compiler_params=_params(("arbitrary",)),
        name="router",
    )(x, nw, sh, sc, rw)


def _select_kernel(a_ref, pose_ref, post_ref, off_ref, *, cap):
    n = a_ref.shape[1]
    nb = n // LANES
    def search(it, thr):
        bits = pltpu.bitcast(a_ref[...], I32)
        cand = thr | lax.shift_left(jnp.int32(1), 30 - it)
        cnt = jnp.sum(jnp.where(bits >= cand, 1.0, 0.0), axis=1, keepdims=True)
        return jnp.where(cnt >= cap, cand, thr)

    thr = lax.fori_loop(0, 31, search, jnp.zeros((N_EXPERTS, 1), I32))
    n_gt = jnp.sum(jnp.where(pltpu.bitcast(a_ref[...], I32) > thr, 1.0, 0.0), axis=1, keepdims=True)
    need = cap - n_gt

    tri = jnp.where(lax.broadcasted_iota(I32, (LANES, LANES), 0) <= lax.broadcasted_iota(I32, (LANES, LANES), 1),
                    1.0, 0.0).astype(BF16)

    def block(b, carry):
        c_eq, c_sel = carry
        start = pl.multiple_of(b * LANES, LANES)
        bb = pltpu.bitcast(a_ref[:, pl.ds(start, LANES)], I32)
        gt = jnp.where(bb > thr, 1.0, 0.0)
        eq = jnp.where(bb == thr, 1.0, 0.0)
        inc_eq = _dot(eq.astype(BF16), tri)
        rank_eq = c_eq + inc_eq - eq
        sel = gt + eq * jnp.where(rank_eq < need, 1.0, 0.0)
        inc_sel = _dot(sel.astype(BF16), tri)
        pos = jnp.where(sel > 0.0, c_sel + inc_sel - sel, -1.0)
        pose_ref[:, pl.ds(start, LANES)] = pos.astype(I32)
        full = jnp.concatenate([pos, jnp.full((LANES - N_EXPERTS, LANES), -1.0, F32)], axis=0)
        post_ref[pl.ds(start, LANES), :] = full.T
        off_ref[b] = jnp.broadcast_to(c_sel, (N_EXPERTS, LANES))
        return c_eq + inc_eq[:, LANES - 1:LANES], c_sel + inc_sel[:, LANES - 1:LANES]

    zero = jnp.zeros((N_EXPERTS, 1), F32)
    lax.fori_loop(0, nb, block, (zero, zero))


def _select(afft, cap):
    n = afft.shape[1]
    nb = n // LANES
    return pl.pallas_call(
        functools.partial(_select_kernel, cap=cap),
        out_shape=[jax.ShapeDtypeStruct((N_EXPERTS, n), I32), jax.ShapeDtypeStruct((n, LANES), F32),
                   jax.ShapeDtypeStruct((nb, N_EXPERTS, LANES), F32)],
        compiler_params=_params(None),
        name="select",
    )(afft)


def _gather_kernel(jt_ref, tt_ref, fl_ref, pos_ref, h_ref, o_ref, acc_scr, *, sb, kp):
    e = pl.program_id(0)
    k = pl.program_id(1)
    idx = e * kp + k
    j = jt_ref[idx]
    flags = fl_ref[idx]

    @pl.when((flags & 1) != 0)
    def _():
        pos = pos_ref[0]
        tb = pos.shape[1]
        slot = lax.broadcasted_iota(I32, (sb, tb), 0) + j * sb
        onehot = jnp.where(pos == slot, 1.0, 0.0).astype(BF16)
        rows = _dot(onehot, h_ref[...])

        @pl.when((flags & 2) != 0)
        def _():
            acc_scr[...] = rows

        @pl.when((flags & 2) == 0)
        def _():
            acc_scr[...] = acc_scr[...] + rows

    @pl.when((flags & 4) != 0)
    def _():
        o_ref[0] = acc_scr[...].astype(o_ref.dtype)


def _gather(jt, tt, fl, pose, h2, cap, sb, tb, kp):
    n = h2.shape[0]
    nt = n // tb
    pos3 = pose.reshape(N_EXPERTS * nt, 1, tb)
    grid_spec = pltpu.PrefetchScalarGridSpec(
        num_scalar_prefetch=3,
        grid=(N_EXPERTS, kp),
        in_specs=[
            pl.BlockSpec((1, 1, tb), lambda e, k, jt, tt, fl: (e * nt + tt[e * kp + k], 0, 0)),
            pl.BlockSpec((tb, D), lambda e, k, jt, tt, fl: (tt[e * kp + k], 0)),
        ],
        out_specs=pl.BlockSpec((1, sb, D), lambda e, k, jt, tt, fl: (e, jt[e * kp + k], 0)),
        scratch_shapes=[pltpu.VMEM((sb, D), F32)],
    )
    return pl.pallas_call(
        functools.partial(_gather_kernel, sb=sb, kp=kp),
        grid_spec=grid_spec,
        out_shape=jax.ShapeDtypeStruct((N_EXPERTS, cap, D), BF16),
        compiler_params=_params(("arbitrary", "arbitrary")),
        name="gather",
    )(jt, tt, fl, pos3, h2)


def _ffn_kernel(x_ref, w1_ref, w3_ref, w2_ref, o_ref, acc_scr):
    f = pl.program_id(1)
    nf = pl.num_programs(1)
    x = x_ref[0]
    h1 = _dot(x, w1_ref[0, 0].astype(BF16))
    h3 = _dot(x, w3_ref[0, 0].astype(BF16))
    he = (_silu(h1) * h3).astype(BF16)
    contrib = _dot(he, w2_ref[0, 0].astype(BF16))

    @pl.when(f == 0)
    def _():
        acc_scr[...] = contrib

    @pl.when(f > 0)
    def _():
        acc_scr[...] = acc_scr[...] + contrib

    @pl.when(f == nf - 1)
    def _():
        o_ref[0] = acc_scr[...].astype(o_ref.dtype)


def _ffn(xe, w1, w3, w2, l):
    cap = xe.shape[1]
    fc = 256
    return pl.pallas_call(
        _ffn_kernel,
        grid=(N_EXPERTS, EXPERT_FF // fc),
        in_specs=[
            pl.BlockSpec((1, cap, D), lambda e, f: (e, 0, 0)),
            pl.BlockSpec((1, 1, D, fc), lambda e, f: (l, e, 0, f)),
            pl.BlockSpec((1, 1, D, fc), lambda e, f: (l, e, 0, f)),
            pl.BlockSpec((1, 1, fc, D), lambda e, f: (l, e, f, 0)),
        ],
        out_specs=pl.BlockSpec((1, cap, D), lambda e, f: (e, 0, 0)),
        out_shape=jax.ShapeDtypeStruct((N_EXPERTS, cap, D), BF16),
        scratch_shapes=[pltpu.VMEM((cap, D), F32)],
        compiler_params=_params(("arbitrary", "arbitrary")),
        name="ffn",
    )(xe, w1, w3, w2)


def _combine_kernel(j0_ref, pos_ref, gate_ref, ya_ref, yb_ref, x_ref, m5_ref, fw_ref, o_ref, acc_scr, *, sb, final):
    b = pl.program_id(0)
    e = pl.program_id(1)
    j0 = j0_ref[b * N_EXPERTS + e]
    tb = pos_ref.shape[0]
    lane = lax.broadcasted_iota(I32, (tb, LANES), 1)
    mine = lane == e
    posc = jnp.sum(jnp.where(mine, pos_ref[...], 0.0), axis=1, keepdims=True)
    gc = jnp.sum(jnp.where(mine, gate_ref[...], 0.0), axis=1, keepdims=True)
    rel = posc - (j0 * sb).astype(F32)
    slot = lax.broadcasted_iota(I32, (tb, sb), 1).astype(F32)
    s_a = jnp.where(rel == slot, 1.0, 0.0).astype(BF16)
    s_b = jnp.where(rel - sb == slot, 1.0, 0.0).astype(BF16)
    contrib = (_dot(s_a, ya_ref[0]) + _dot(s_b, yb_ref[0])) * gc

    @pl.when(e == 0)
    def _():
        acc_scr[...] = contrib

    @pl.when(e > 0)
    def _():
        acc_scr[...] = acc_scr[...] + contrib

    @pl.when(e == N_EXPERTS - 1)
    def _():
        y = x_ref[...] + m5_ref[...] * acc_scr[...]
        if final:
            y = y * lax.rsqrt(jnp.mean(y * y, axis=-1, keepdims=True) + EPS) * fw_ref[...]
        o_ref[...] = y


def _combine(j0, post, aff, ye, x, m5, fw, sb, tb, final):
    n = x.shape[0]
    cap = ye.shape[1]
    nj = cap // sb
    vec = pl.BlockSpec((1, D), lambda b, e, j0: (0, 0))
    grid_spec = pltpu.PrefetchScalarGridSpec(
        num_scalar_prefetch=1,
        grid=(n // tb, N_EXPERTS),
        in_specs=[
            pl.BlockSpec((tb, LANES), lambda b, e, j0: (b, 0)),
            pl.BlockSpec((tb, LANES), lambda b, e, j0: (b, 0)),
            pl.BlockSpec((1, sb, D), lambda b, e, j0: (e, j0[b * N_EXPERTS + e], 0)),
            pl.BlockSpec((1, sb, D), lambda b, e, j0: (e, jnp.minimum(j0[b * N_EXPERTS + e] + 1, nj - 1), 0)),
            pl.BlockSpec((tb, D), lambda b, e, j0: (b, 0)),
            vec, vec,
        ],
        out_specs=pl.BlockSpec((tb, D), lambda b, e, j0: (b, 0)),
        scratch_shapes=[pltpu.VMEM((tb, D), F32)],
    )
    return pl.pallas_call(
        functools.partial(_combine_kernel, sb=sb, final=final),
        grid_spec=grid_spec,
        out_shape=jax.ShapeDtypeStruct((n, D), F32),
        compiler_params=_params(("arbitrary", "arbitrary")),
        name="combine",
    )(j0, post, aff, ye, ye, x, m5, fw)


def _pair_tables(off_tb, cap, sb, kp):
    ne, nt = off_tb.shape
    nxt = jnp.concatenate([off_tb[:, 1:], jnp.full((ne, 1), cap, I32)], axis=1)
    cnt = nxt - off_tb
    j_lo = off_tb // sb
    j_hi = jnp.where(cnt > 0, (nxt - 1) // sb, j_lo - 1)
    npair = j_hi - j_lo + 1
    starts = jnp.cumsum(npair, axis=1) - npair
    total = jnp.sum(npair, axis=1, keepdims=True)
    k = jnp.arange(kp, dtype=I32)[None, :]
    kk = jnp.minimum(k, total - 1)
    tb_k = jnp.sum((starts[:, None, :] <= kk[:, :, None]).astype(I32), axis=2) - 1
    j_k = jnp.take_along_axis(j_lo, tb_k, axis=1) + kk - jnp.take_along_axis(starts, tb_k, axis=1)
    valid = (k < total).astype(I32)
    prev_j = jnp.concatenate([jnp.full((ne, 1), -1, I32), j_k[:, :-1]], axis=1)
    first = ((j_k != prev_j) & (k < total)).astype(I32)
    next_j = jnp.concatenate([j_k[:, 1:], jnp.full((ne, 1), -1, I32)], axis=1)
    last = (((j_k != next_j) | (k + 1 >= total)) & (k < total)).astype(I32)
    flags = valid + 2 * first + 4 * last
    return j_k.reshape(-1).astype(I32), tb_k.reshape(-1).astype(I32), flags.reshape(-1).astype(I32)


def _moe(x, nw, sh, sc, m5, rw, w1, w3, w2, l, fw, final):
    n = x.shape[0]
    cap = CAPACITY_FACTOR * n // N_EXPERTS
    sb = min(256, cap)
    tbg = min(1024, n)
    tbc = min(256, n)
    nj = cap // sb
    assert nj == 1 or tbc <= sb
    kp = nj + n // tbg
    h2, aff, afft = _router(x, nw, sh, sc, rw)
    pose, post, off = _select(afft, cap)
    off128 = off[:, :, 0].T.astype(I32)
    jt, tt, fl = _pair_tables(off128[:, :: tbg // LANES], cap, sb, kp)
    xe = _gather(jt, tt, fl, pose, h2, cap, sb, tbg, kp)
    ye = _ffn(xe, w1, w3, w2, l)
    j0 = (off128[:, :: tbc // LANES] // sb).T.reshape(-1).astype(I32)
    return _combine(j0, post, aff, ye, x, m5, fw, sb, tbc, final)


def _rope_tables(n_rows):
    row = jnp.repeat(jnp.arange(n_rows, dtype=F32), GRID_W)
    col = jnp.tile(jnp.arange(GRID_W, dtype=F32), n_rows)
    half = DIFF_HEAD_DIM // 2
    freqs = ROPE_BASE ** (-jnp.arange(0, half, 2, dtype=F32) / half)
    ar, ac = row[:, None] * freqs, col[:, None] * freqs
    cr, sr, cc, sn = jnp.cos(ar), jnp.sin(ar), jnp.cos(ac), jnp.sin(ac)
    z = jnp.zeros_like(sr)
    cos = jnp.concatenate([cr, cr, cc, cc], axis=1)
    sa = jnp.concatenate([-sr, z, -sn, z], axis=1)
    sb = jnp.concatenate([z, sr, z, sn], axis=1)
    return tuple(jnp.tile(t, (1, 2)) for t in (cos, sa, sb))


def _identity_tables(n):
    return (jnp.ones((n, LANES), F32), jnp.zeros((n, LANES), F32), jnp.zeros((n, LANES), F32))


def _proj_weight(w_in):
    sizes = (1024, 256, 32, 1024, 1024, 256, 1024, 1024, 2048, 3072)
    idx = np.cumsum((0,) + sizes)
    px, pb, pdt, pk, pv, pc, pq, pz, pglu, pgate = [w_in[:, idx[i]:idx[i + 1]] for i in range(10)]
    pad = jnp.zeros((D, LANES - SSD_HEADS), w_in.dtype)
    tail = jnp.zeros((D, N_PROJ - OFF_DT - 2 * LANES), w_in.dtype)
    w = jnp.concatenate([px, pz, pk, pv, pq, pglu, pgate, pb, pc,
                         pdt[:, :SSD_HEADS], pad, pdt[:, SSD_HEADS:], pad, tail], axis=1)
    return w.astype(BF16)


def _lane_pad(v):
    return jnp.pad(v, ((0, 0), (0, LANES - v.shape[1]))).reshape(2, 1, LANES)


def _mixer_front(x, nw, sh, sc, w, conv_w, conv_b):
    p = _inproj(x, nw, sh, sc, w)
    xbc = _ssdconv(p, conv_w, conv_b)
    return p, xbc


def kernel(x, c, ctx, c_ctx, ada_w, ada_b, norm1_w, norm2_w, w_in, ssd_conv_w, ssd_conv_b, ssd_dt_bias, ssd_a_log, ssd_d, ssd_norm_w, ssd_out, diff_lambda, diff_subln_w, diff_out, conf_dw_w, conf_dw_b, conf_ln_w, conf_ln_b, conf_out, w_o, router_w, exp_w1, exp_w3, exp_w2, final_norm_w):
    depth = ada_w.shape[0]
    n = x.shape[1]
    nctx = ctx.shape[1]
    x_lat = x[0]
    x_ctx = ctx[0]

    cc = jnp.zeros((8, D), F32).at[0].set(c[0]).at[1].set(c_ctx)
    mods = _ada(cc, ada_w, ada_b)

    rope = _rope_tables(n // GRID_W)
    ident_ctx = _identity_tables(nctx)
    scale = DIFF_HEAD_DIM ** -0.5 * math.log2(math.e)
    zero_state = jnp.zeros((2, SSD_STATE, SSD_HEADS * SSD_HEAD_DIM), F32)
    fw = final_norm_w.reshape(1, D)

    for l in range(depth):
        last = l == depth - 1
        lam_init = 0.8 - 0.6 * math.exp(-0.3 * l)
        lq1, lk1, lq2, lk2 = diff_lambda[l].astype(F32)
        lam = (jnp.exp(jnp.sum(lq1 * lk1)) - jnp.exp(jnp.sum(lq2 * lk2)) + lam_init).reshape(1)
        m_lat = [mods[l, 0:1, i * D:(i + 1) * D] for i in range(N_MOD)]
        m_ctx = [mods[l, 1:2, i * D:(i + 1) * D] for i in range(N_MOD)]
        nw1 = norm1_w[l].reshape(1, D)
        nw2 = norm2_w[l].reshape(1, D)
        w = _proj_weight(w_in[l])
        dtb = _lane_pad(ssd_dt_bias[l])
        alog = _lane_pad(ssd_a_log[l])
        dsk = jnp.repeat(ssd_d[l], SSD_HEAD_DIM).reshape(1, D)
        gnw = ssd_norm_w[l].reshape(1, D)
        sw = diff_subln_w[l].reshape(1, LANES)
        wa, wb, wc, wo = (t[l].astype(BF16) for t in (ssd_out, diff_out, conf_out, w_o))
        rw = jnp.pad(router_w[l], ((0, 0), (0, LANES - N_EXPERTS))).astype(BF16)
        coef = 1.0 - lam_init

        p_c, xbc_c = _mixer_front(x_ctx, nw1, m_ctx[0], m_ctx[1], w, ssd_conv_w[l], ssd_conv_b[l])
        y2_c, st_c = _ssd(xbc_c, p_c, dtb, alog, zero_state)
        p_l, xbc_l = _mixer_front(x_lat, nw1, m_lat[0], m_lat[1], w, ssd_conv_w[l], ssd_conv_b[l])
        k_all, v_all = _kvprep(p_c, p_l, rope)

        if not last:
            q_c = _qprep(p_c, ident_ctx, scale)
            on_c = _attention(lam, q_c, k_all, v_all, sw, nctx, coef)
            cn_c = _conformer(p_c, conf_dw_w[l], conf_dw_b[l], conf_ln_w[l], conf_ln_b[l])
            x_ctx = _merge(y2_c, xbc_c, p_c, on_c, cn_c, x_ctx, dsk, gnw, m_ctx[2], wa, wb, wc, wo)
            x_ctx = _moe(x_ctx, nw2, m_ctx[3], m_ctx[4], m_ctx[5], rw, exp_w1, exp_w3, exp_w2, l, fw, False)

        y2_l, _ = _ssd(xbc_l, p_l, dtb, alog, st_c)
        q_l = _qprep(p_l, rope, scale)
        on_l = _attention(lam, q_l, k_all, v_all, sw, nctx + n, coef)
        cn_l = _conformer(p_l, conf_dw_w[l], conf_dw_b[l], conf_ln_w[l], conf_ln_b[l])
        x_lat = _merge(y2_l, xbc_l, p_l, on_l, cn_l, x_lat, dsk, gnw, m_lat[2], wa, wb, wc, wo)
        x_lat = _moe(x_lat, nw2, m_lat[3], m_lat[4], m_lat[5], rw, exp_w1, exp_w3, exp_w2, l, fw, last)

    return x_lat[None]
```

```python
import functools
import math

import jax
import jax.numpy as jnp
import numpy as np
from jax import lax
from jax.experimental import pallas as pl
from jax.experimental.pallas import tpu as pltpu

F32 = jnp.float32
BF16 = jnp.bfloat16
I32 = jnp.int32

EPS = 1e-6
D = 1024
N_MOD = 6
GRID_W = 64
SSD_HEADS = 16
SSD_HEAD_DIM = 64
SSD_STATE = 128
SSD_GN = 256
SSD_CHUNK = 128
DIFF_HEADS = 8
DIFF_HEAD_DIM = 64
ROPE_BASE = 10000.0
CONF_KERNEL = 31
CONF_HALO = 16
N_EXPERTS = 16
EXPERT_FF = 2048
CAPACITY_FACTOR = 2
LANES = 128
BF16_ROWS = 16
COMBINE_WIN = 128

OFF_X, OFF_Z, OFF_K, OFF_V, OFF_Q = 0, 1024, 2048, 3072, 4096
OFF_GLU_A, OFF_GLU_G, OFF_GATE = 5120, 6144, 7168
OFF_B, OFF_C, OFF_DT = 10240, 10496, 10752
N_PROJ = 11264

VMEM_LIMIT = 56 * 1024 * 1024


def _params(sem, vmem=VMEM_LIMIT):
    return pltpu.CompilerParams(dimension_semantics=sem, vmem_limit_bytes=vmem)


def _silu(x):
    return x * jax.nn.sigmoid(x)


def _split3(x):
    a1 = x.astype(BF16)
    r1 = x - a1.astype(F32)
    a2 = r1.astype(BF16)
    a3 = (r1 - a2.astype(F32)).astype(BF16)
    return a1, a2, a3


def _dot(a, b):
    return jnp.dot(a, b, preferred_element_type=F32)


def _ada_kernel(c_ref, w_ref, b_ref, o_ref):
    a = _silu(c_ref[...]).astype(BF16)
    o_ref[0] = _dot(a, w_ref[0].astype(BF16)) + b_ref[0]


def _ada(cc, ada_w, ada_b):
    nl = ada_w.shape[0]
    tn = 1536
    return pl.pallas_call(
        _ada_kernel,
        grid=(nl, N_MOD * D // tn),
        in_specs=[
            pl.BlockSpec((8, D), lambda l, j: (0, 0)),
            pl.BlockSpec((1, D, tn), lambda l, j: (l, 0, j)),
            pl.BlockSpec((1, 1, tn), lambda l, j: (l, 0, j)),
        ],
        out_specs=pl.BlockSpec((1, 8, tn), lambda l, j: (l, 0, j)),
        out_shape=jax.ShapeDtypeStruct((nl, 8, N_MOD * D), F32),
        compiler_params=_params(("arbitrary", "arbitrary")),
        name="ada",
    )(cc, ada_w, ada_b.reshape(nl, 1, N_MOD * D))


def _modulate(x, nw, sh, sc):
    r = lax.rsqrt(jnp.mean(x * x, axis=-1, keepdims=True) + EPS)
    return (x * r * nw) * (1.0 + sc) + sh


def _inproj_kernel(x_ref, nw_ref, sh_ref, sc_ref, w_ref, o_ref, h_scr):
    @pl.when(pl.program_id(1) == 0)
    def _():
        h_scr[...] = _modulate(x_ref[...], nw_ref[...], sh_ref[...], sc_ref[...]).astype(BF16)

    o_ref[...] = _dot(h_scr[...], w_ref[...])


def _inproj(x, nw, sh, sc, w):
    n = x.shape[0]
    tm = min(n, 1024)
    tn = 1024
    vec = pl.BlockSpec((1, D), lambda i, j: (0, 0))
    return pl.pallas_call(
        _inproj_kernel,
        grid=(n // tm, N_PROJ // tn),
        in_specs=[pl.BlockSpec((tm, D), lambda i, j: (i, 0)), vec, vec, vec,
                  pl.BlockSpec((D, tn), lambda i, j: (0, j))],
        out_specs=pl.BlockSpec((tm, tn), lambda i, j: (i, j)),
        out_shape=jax.ShapeDtypeStruct((n, N_PROJ), F32),
        scratch_shapes=[pltpu.VMEM((tm, D), BF16)],
        compiler_params=_params(("arbitrary", "arbitrary")),
        name="inproj",
    )(x, nw, sh, sc, w)


def _ssdconv_kernel(prev_ref, cur_ref, next_ref, w_ref, b_ref, o_ref):
    i = pl.program_id(0)
    nt = pl.num_programs(0)
    u = cur_ref[...]
    t = u.shape[0]
    row = lax.broadcasted_iota(I32, u.shape, 0)
    before = jnp.where(i > 0, prev_ref[7:8, :], 0.0)
    after = jnp.where(i < nt - 1, next_ref[0:1, :], 0.0)
    um1 = jnp.where(row == 0, before, pltpu.roll(u, 1, 0))
    up1 = jnp.where(row == t - 1, after, pltpu.roll(u, t - 1, 0))
    y = um1 * w_ref[0:1, :] + u * w_ref[1:2, :] + up1 * w_ref[2:3, :] + b_ref[...]
    o_ref[...] = _silu(y)


def _ssdconv(p, conv_w, conv_b):
    n = p.shape[0]
    t = min(n, 512)
    cw = 512
    r8 = t // 8
    nb8 = n // 8

    def col(c):
        return jnp.where(c < 2, c, OFF_B // cw)

    return pl.pallas_call(
        _ssdconv_kernel,
        grid=(n // t, 3),
        in_specs=[
            pl.BlockSpec((8, cw), lambda i, c: (jnp.maximum(i * r8 - 1, 0), col(c))),
            pl.BlockSpec((t, cw), lambda i, c: (i, col(c))),
            pl.BlockSpec((8, cw), lambda i, c: (jnp.minimum((i + 1) * r8, nb8 - 1), col(c))),
            pl.BlockSpec((3, cw), lambda i, c: (0, c)),
            pl.BlockSpec((1, cw), lambda i, c: (0, c)),
        ],
        out_specs=pl.BlockSpec((t, cw), lambda i, c: (i, c)),
        out_shape=jax.ShapeDtypeStruct((n, 3 * cw), F32),
        compiler_params=_params(("arbitrary", "arbitrary")),
        name="ssdconv",
    )(p, p, p, conv_w, conv_b.reshape(1, -1))


def _ssd_direction(d, xs_ref, b_ref, c_ref, dt_ref, dtb, alog, st_ref, y_ref):
    L = SSD_CHUNK
    P = SSD_HEAD_DIM
    HP = SSD_HEADS * P
    GW = HP // 2

    x = dt_ref[...] + dtb
    dt = jnp.maximum(x, 0.0) + jnp.log1p(jnp.exp(-jnp.abs(x)))
    a = dt * (-jnp.exp(alog))

    li = lax.broadcasted_iota(I32, (L, L), 0)
    si = lax.broadcasted_iota(I32, (L, L), 1)
    keep = si <= li if d == 0 else si >= li
    tri = jnp.where(keep, 1.0, 0.0).astype(BF16)
    a1, a2, a3 = _split3(a)
    cum = _dot(tri, a1) + _dot(tri, a2) + _dot(tri, a3)
    cum_t = cum.T

    ex = jnp.where(lax.broadcasted_iota(I32, (LANES, HP), 0) == lax.broadcasted_iota(I32, (LANES, HP), 1) // P,
                   1.0, 0.0).astype(BF16)

    def expand(v):
        v1, v2, v3 = _split3(v)
        return _dot(v1, ex) + _dot(v2, ex) + _dot(v3, ex)

    cum_e = expand(cum)
    dt_e = expand(dt)
    tot_e = cum_e[L - 1:L, :] if d == 0 else cum_e[0:1, :]

    xdt = xs_ref[...] * dt_e
    xb = xdt.astype(BF16)
    xw = (xdt * jnp.exp(tot_e - cum_e)).astype(BF16)
    bb = b_ref[...].astype(BF16)
    cb = c_ref[...].astype(BF16)
    st = st_ref[...]
    stb = st.astype(BF16)
    grow = jnp.exp(cum_e)

    for g in range(2):
        bg = bb[:, g * SSD_STATE:(g + 1) * SSD_STATE]
        cg = cb[:, g * SSD_STATE:(g + 1) * SSD_STATE]
        scores = lax.dot_general(cg, bg, (((1,), (1,)), ((), ())), preferred_element_type=F32)
        y_off = _dot(cg, stb[:, g * GW:(g + 1) * GW]) * grow[:, g * GW:(g + 1) * GW]
        for hh in range(SSD_HEADS // 2):
            h = g * (SSD_HEADS // 2) + hh
            seg = cum[:, h:h + 1] - cum_t[h:h + 1, :]
            dec = jnp.where(keep, jnp.exp(jnp.where(keep, seg, 0.0)), 0.0)
            m = (scores * dec).astype(BF16)
            yd = _dot(m, xb[:, h * P:(h + 1) * P])
            y_ref[:, h * P:(h + 1) * P] = yd + y_off[:, hh * P:(hh + 1) * P]
        upd = lax.dot_general(bg, xw[:, g * GW:(g + 1) * GW], (((0,), (0,)), ((), ())),
                              preferred_element_type=F32)
        st_ref[:, g * GW:(g + 1) * GW] = st[:, g * GW:(g + 1) * GW] * jnp.exp(tot_e[:, g * GW:(g + 1) * GW]) + upd


def _ssd_kernel(xsf_ref, bf_ref, cf_ref, dtf_ref, xsb_ref, bb_ref, cb_ref, dtb_ref, bias_ref, alog_ref, h0_ref,
                yf_ref, yb_ref, hT_ref, st_scr):
    c = pl.program_id(0)

    @pl.when(c == 0)
    def _():
        st_scr[...] = h0_ref[...]

    _ssd_direction(0, xsf_ref, bf_ref, cf_ref, dtf_ref, bias_ref[0], alog_ref[0], st_scr.at[0], yf_ref)
    _ssd_direction(1, xsb_ref, bb_ref, cb_ref, dtb_ref, bias_ref[1], alog_ref[1], st_scr.at[1], yb_ref)

    @pl.when(c == pl.num_programs(0) - 1)
    def _():
        hT_ref[...] = st_scr[...]


def _ssd(xbc, p, dtb, alog, h0):
    n = xbc.shape[0]
    L = SSD_CHUNK
    nc = n // L
    HP = SSD_HEADS * SSD_HEAD_DIM
    full = pl.BlockSpec((2, SSD_STATE, HP), lambda c: (0, 0, 0))
    vec = pl.BlockSpec((2, 1, LANES), lambda c: (0, 0, 0))

    def side(chunk, d):
        return [
            pl.BlockSpec((L, HP), lambda c: (chunk(c), 0)),
            pl.BlockSpec((L, SSD_GN), lambda c: (chunk(c), HP // SSD_GN)),
            pl.BlockSpec((L, SSD_GN), lambda c: (chunk(c), HP // SSD_GN + 1)),
            pl.BlockSpec((L, LANES), lambda c: (chunk(c), OFF_DT // LANES + d)),
        ]

    def fwd(c):
        return c

    def bwd(c):
        return nc - 1 - c

    return pl.pallas_call(
        _ssd_kernel,
        grid=(nc,),
        in_specs=side(fwd, 0) + side(bwd, 1) + [vec, vec, full],
        out_specs=[pl.BlockSpec((L, HP), lambda c: (c, 0)), pl.BlockSpec((L, HP), lambda c: (nc - 1 - c, 0)), full],
        out_shape=[jax.ShapeDtypeStruct((n, HP), F32), jax.ShapeDtypeStruct((n, HP), F32),
                   jax.ShapeDtypeStruct((2, SSD_STATE, HP), F32)],
        scratch_shapes=[pltpu.VMEM((2, SSD_STATE, HP), F32)],
        compiler_params=_params(("arbitrary",)),
        name="ssd",
    )(xbc, xbc, xbc, p, xbc, xbc, xbc, p, dtb, alog, h0)


def _rope_tile(x_ref, o_ref, cos, sa, sb, scale):
    for j in range(D // LANES):
        xj = x_ref[:, j * LANES:(j + 1) * LANES]
        yj = xj * cos + pltpu.roll(xj, LANES - 16, 1) * sa + pltpu.roll(xj, 16, 1) * sb
        o_ref[:, j * LANES:(j + 1) * LANES] = (yj * scale).astype(o_ref.dtype)


def _qprep_kernel(q_ref, cos_ref, sa_ref, sb_ref, o_ref, *, scale):
    _rope_tile(q_ref, o_ref, cos_ref[...], sa_ref[...], sb_ref[...], scale)


def _qprep(p, tabs, scale):
    n = p.shape[0]
    t = min(n, 256)
    tab = pl.BlockSpec((t, LANES), lambda i: (i, 0))
    return pl.pallas_call(
        functools.partial(_qprep_kernel, scale=scale),
        grid=(n // t,),
        in_specs=[pl.BlockSpec((t, D), lambda i: (i, OFF_Q // D)), tab, tab, tab],
        out_specs=pl.BlockSpec((t, D), lambda i: (i, 0)),
        out_shape=jax.ShapeDtypeStruct((n, D), BF16),
        compiler_params=_params(("arbitrary",)),
        name="qprep",
    )(p, *tabs)


def _kvprep_kernel(kc_ref, vc_ref, kl_ref, vl_ref, cos_ref, sa_ref, sb_ref, ko_ref, vo_ref):
    i = pl.program_id(0)

    @pl.when(i == 0)
    def _():
        ko_ref[...] = kc_ref[...].astype(BF16)
        vo_ref[...] = vc_ref[...].astype(BF16)

    @pl.when(i > 0)
    def _():
        _rope_tile(kl_ref, ko_ref, cos_ref[...], sa_ref[...], sb_ref[...], 1.0)
        vo_ref[...] = vl_ref[...].astype(BF16)


def _kvprep(p_ctx, p_lat, tabs):
    nctx = p_ctx.shape[0]
    n = p_lat.shape[0]
    t = nctx
    m = nctx + n

    def lat(i):
        return jnp.maximum(i - 1, 0)

    tab = pl.BlockSpec((t, LANES), lambda i: (lat(i), 0))
    out = pl.BlockSpec((t, D), lambda i: (i, 0))
    return pl.pallas_call(
        _kvprep_kernel,
        grid=(m // t,),
        in_specs=[
            pl.BlockSpec((t, D), lambda i: (0, OFF_K // D)),
            pl.BlockSpec((t, D), lambda i: (0, OFF_V // D)),
            pl.BlockSpec((t, D), lambda i: (lat(i), OFF_K // D)),
            pl.BlockSpec((t, D), lambda i: (lat(i), OFF_V // D)),
            tab, tab, tab,
        ],
        out_specs=[out, out],
        out_shape=[jax.ShapeDtypeStruct((m, D), BF16), jax.ShapeDtypeStruct((m, D), BF16)],
        compiler_params=_params(("arbitrary",)),
        name="kvprep",
    )(p_ctx, p_ctx, p_lat, p_lat, *tabs)


def _attn_kernel(lam_ref, q_ref, k_ref, v_ref, sw_ref, o_ref, q2_scr, sa_scr, sb_scr, p_scr, m_scr, acc_scr,
                 *, tq, nq, ck, nchunks, coef):
    lane = lax.broadcasted_iota(I32, (tq, LANES), 1)
    for t in range(nq):
        q = q_ref[t * tq:(t + 1) * tq, :]
        zero = jnp.zeros_like(q)
        q2_scr[t, 0:tq, :] = jnp.where(lane < DIFF_HEAD_DIM, q, zero)
        q2_scr[t, tq:2 * tq, :] = jnp.where(lane >= DIFF_HEAD_DIM, q, zero)
    m_scr[...] = jnp.full(m_scr.shape, -jnp.inf, F32)
    acc_scr[...] = jnp.zeros(acc_scr.shape, F32)
    ones = jnp.ones((ck, LANES), BF16)
    nb = ck // LANES

    def key_rows(c):
        return pl.ds(c * ck, ck) if isinstance(c, int) else pl.ds(pl.multiple_of(c * ck, ck), ck)

    def scores(t, c, s_scr):
        s_scr[...] = lax.dot_general(q2_scr[t], k_ref[key_rows(c), :], (((1,), (1,)), ((), ())),
                                     preferred_element_type=F32)

    def soft_pv(t, c, s_scr):
        rows = key_rows(c)
        mx = s_scr[:, 0:LANES]
        for b in range(1, nb):
            mx = jnp.maximum(mx, s_scr[:, b * LANES:(b + 1) * LANES])
        m_prev = m_scr[t]
        m_new = jnp.maximum(m_prev, jnp.max(mx, axis=1, keepdims=True))
        alpha = jnp.exp2(m_prev - m_new)
        for b in range(nb):
            p_scr[:, b * LANES:(b + 1) * LANES] = jnp.exp2(s_scr[:, b * LANES:(b + 1) * LANES] - m_new).astype(BF16)
        va = jnp.concatenate([v_ref[rows, :], ones], axis=1)
        acc_scr[t] = acc_scr[t] * jnp.concatenate([alpha, alpha], axis=1) + _dot(p_scr[...], va)
        m_scr[t] = m_new

    def tile(t, cur, nxt):
        def pair(jj, carry):
            scores(t, 2 * jj + 1, nxt)
            soft_pv(t, 2 * jj, cur)
            scores(t, 2 * jj + 2, cur)
            soft_pv(t, 2 * jj + 1, nxt)
            return carry

        lax.fori_loop(0, (nchunks - 1) // 2, pair, 0)
        last, free = cur, nxt
        if (nchunks - 1) % 2 == 1:
            scores(t, nchunks - 1, nxt)
            soft_pv(t, nchunks - 2, cur)
            last, free = nxt, cur
        if t + 1 < nq:
            scores(t + 1, 0, free)
        soft_pv(t, nchunks - 1, last)
        return free, last

    scores(0, 0, sa_scr)
    bufs = (sa_scr, sb_scr)
    for t in range(nq):
        bufs = tile(t, *bufs)

    for t in range(nq):
        on = acc_scr[t, :, 0:LANES] / acc_scr[t, :, LANES:2 * LANES]
        o = on[0:tq, :] - lam_ref[0] * on[tq:2 * tq, :]
        r = lax.rsqrt(jnp.mean(o * o, axis=-1, keepdims=True) + EPS)
        o_ref[t * tq:(t + 1) * tq, :] = ((o * r * sw_ref[...]) * coef).astype(o_ref.dtype)


def _pick_tk(m):
    for cand in (1280, 1024, 512, 256):
        if m % cand == 0:
            return cand
    return m


def _attention(lam, q, k, v, subln_w, m, coef):
    n = q.shape[0]
    tq = min(n, 256)
    nq = min(4, n // tq)
    ck = _pick_tk(m)
    return pl.pallas_call(
        functools.partial(_attn_kernel, tq=tq, nq=nq, ck=ck, nchunks=m // ck, coef=coef),
        grid=(DIFF_HEADS, n // (nq * tq)),
        in_specs=[
            pl.BlockSpec(memory_space=pltpu.SMEM),
            pl.BlockSpec((nq * tq, LANES), lambda h, i: (i, h)),
            pl.BlockSpec((m, LANES), lambda h, i: (0, h)),
            pl.BlockSpec((m, LANES), lambda h, i: (0, h)),
            pl.BlockSpec((1, LANES), lambda h, i: (0, 0)),
        ],
        out_specs=pl.BlockSpec((nq * tq, LANES), lambda h, i: (i, h)),
        out_shape=jax.ShapeDtypeStruct((n, D), BF16),
        scratch_shapes=[pltpu.VMEM((nq, 2 * tq, LANES), BF16), pltpu.VMEM((2 * tq, ck), F32),
                        pltpu.VMEM((2 * tq, ck), F32), pltpu.VMEM((2 * tq, ck), BF16),
                        pltpu.VMEM((nq, 2 * tq, LANES), F32), pltpu.VMEM((nq, 2 * tq, 2 * LANES), F32)],
        compiler_params=_params(("arbitrary", "arbitrary")),
        name="attn",
    )(lam, q, k, v, subln_w)


def _conf_kernel(ap_ref, gp_ref, a_ref, g_ref, an_ref, gn_ref, w_ref, b_ref, lw_ref, lb_ref, o_ref, ext_scr, sh_scr):
    i = pl.program_id(0)
    nt = pl.num_programs(0)
    t = a_ref.shape[0]
    hl = CONF_HALO

    def glu(a, g):
        return a[...] * jax.nn.sigmoid(g[...])

    ext_scr[0:hl, :] = jnp.where(i > 0, glu(ap_ref, gp_ref), 0.0)
    ext_scr[hl:hl + t, :] = glu(a_ref, g_ref)
    ext_scr[hl + t:2 * hl + t, :] = jnp.where(i < nt - 1, glu(an_ref, gn_ref), 0.0)

    span = t + 2 * hl - 8
    for r in range(1, 8):
        sh_scr[r, 0:span, :] = ext_scr[pl.ds(r, span), :]
    acc = jnp.zeros((t, D), F32) + b_ref[...]
    for k in range(CONF_KERNEL):
        off = k + hl - CONF_KERNEL // 2
        src = ext_scr if off % 8 == 0 else sh_scr.at[off % 8]
        acc = acc + src[pl.ds(off - off % 8, t), :] * w_ref[k:k + 1, :]
    mu = jnp.mean(acc, axis=-1, keepdims=True)
    xc = acc - mu
    y = xc * lax.rsqrt(jnp.mean(xc * xc, axis=-1, keepdims=True) + EPS) * lw_ref[...] + lb_ref[...]
    o_ref[...] = _silu(y).astype(o_ref.dtype)


def _conformer(p, dw_w, dw_b, ln_w, ln_b):
    n = p.shape[0]
    t = min(n, 256)
    hl = CONF_HALO
    rh = t // hl
    nbh = n // hl
    ca, cg = OFF_GLU_A // D, OFF_GLU_G // D

    def prev(i):
        return jnp.maximum(i * rh - 1, 0)

    def nxt(i):
        return jnp.minimum((i + 1) * rh, nbh - 1)

    vec = pl.BlockSpec((1, D), lambda i: (0, 0))
    return pl.pallas_call(
        _conf_kernel,
        grid=(n // t,),
        in_specs=[
            pl.BlockSpec((hl, D), lambda i: (prev(i), ca)), pl.BlockSpec((hl, D), lambda i: (prev(i), cg)),
            pl.BlockSpec((t, D), lambda i: (i, ca)), pl.BlockSpec((t, D), lambda i: (i, cg)),
            pl.BlockSpec((hl, D), lambda i: (nxt(i), ca)), pl.BlockSpec((hl, D), lambda i: (nxt(i), cg)),
            pl.BlockSpec((CONF_KERNEL, D), lambda i: (0, 0)), vec, vec, vec,
        ],
        out_specs=pl.BlockSpec((t, D), lambda i: (i, 0)),
        out_shape=jax.ShapeDtypeStruct((n, D), BF16),
        scratch_shapes=[pltpu.VMEM((t + 2 * hl, D), F32), pltpu.VMEM((8, t + 2 * hl, D), F32)],
        compiler_params=_params(("arbitrary",)),
        name="conformer",
    )(p, p, p, p, p, p, dw_w, dw_b.reshape(1, D), ln_w.reshape(1, D), ln_b.reshape(1, D))


def _merge_kernel(yf_ref, yb_ref, xs_ref, z_ref, on_ref, cn_ref, g0_ref, g1_ref, g2_ref, x_ref, dsk_ref, nw_ref,
                  m2_ref, wa_ref, wb_ref, wc_ref, wo_ref, o_ref):
    y = yf_ref[...] + yb_ref[...] + xs_ref[...] * dsk_ref[...]
    yz = y * _silu(z_ref[...])
    half = D // 2
    parts = []
    for g in range(2):
        seg = yz[:, g * half:(g + 1) * half]
        r = lax.rsqrt(jnp.mean(seg * seg, axis=-1, keepdims=True) + EPS)
        parts.append(seg * r * nw_ref[:, g * half:(g + 1) * half])
    gn = jnp.concatenate(parts, axis=1).astype(BF16)
    y_a = _dot(gn, wa_ref[...])
    y_b = _dot(on_ref[...], wb_ref[...])
    y_c = _dot(cn_ref[...], wc_ref[...])
    m = jax.nn.sigmoid(g0_ref[...]) * y_a + jax.nn.sigmoid(g1_ref[...]) * y_b + jax.nn.sigmoid(g2_ref[...]) * y_c
    y_o = _dot(m.astype(BF16), wo_ref[...])
    o_ref[...] = x_ref[...] + m2_ref[...] * y_o


def _merge(yf, yb, xbc, p, on, cn, x, dsk, nw, m2, wa, wb, wc, wo):
    n = x.shape[0]
    tm = min(n, 256)
    vec = pl.BlockSpec((1, D), lambda i: (0, 0))
    wsp = pl.BlockSpec((D, D), lambda i: (0, 0))
    row = pl.BlockSpec((tm, D), lambda i: (i, 0))
    return pl.pallas_call(
        _merge_kernel,
        grid=(n // tm,),
        in_specs=[
            row, row,
            row,
            pl.BlockSpec((tm, D), lambda i: (i, OFF_Z // D)),
            row, row,
            pl.BlockSpec((tm, D), lambda i: (i, OFF_GATE // D)),
            pl.BlockSpec((tm, D), lambda i: (i, OFF_GATE // D + 1)),
            pl.BlockSpec((tm, D), lambda i: (i, OFF_GATE // D + 2)),
            row, vec, vec, vec, wsp, wsp, wsp, wsp,
        ],
        out_specs=row,
        out_shape=jax.ShapeDtypeStruct((n, D), F32),
        compiler_params=_params(("arbitrary",)),
        name="merge",
    )(yf, yb, xbc, p, on, cn, p, p, p, x, dsk, nw, m2, wa, wb, wc, wo)


def _router_kernel(x_ref, nw_ref, sh_ref, sc_ref, rw_ref, h_ref, aff_ref, afft_ref):
    h = _modulate(x_ref[...], nw_ref[...], sh_ref[...], sc_ref[...]).astype(BF16)
    h_ref[...] = h
    logits = _dot(h, rw_ref[...])
    lane = lax.broadcasted_iota(I32, logits.shape, 1)
    logits = jnp.where(lane < N_EXPERTS, logits, -jnp.inf)
    e = jnp.exp(logits - jnp.max(logits, axis=-1, keepdims=True))
    aff = e / jnp.sum(e, axis=-1, keepdims=True)
    aff_ref[...] = aff
    afft_ref[...] = aff.T[0:N_EXPERTS, :]


def _router(x, nw, sh, sc, rw):
    n = x.shape[0]
    tm = min(n, 512)
    vec = pl.BlockSpec((1, D), lambda i: (0, 0))
    return pl.pallas_call(
        _router_kernel,
        grid=(n // tm,),
        in_specs=[pl.BlockSpec((tm, D), lambda i: (i, 0)), vec, vec, vec, pl.BlockSpec((D, LANES), lambda i: (0, 0))],
        out_specs=[pl.BlockSpec((tm, D), lambda i: (i, 0)), pl.BlockSpec((tm, LANES), lambda i: (i, 0)),
                   pl.BlockSpec((N_EXPERTS, tm), lambda i: (0, i))],
        out_shape=[jax.ShapeDtypeStruct((n, D), BF16), jax.ShapeDtypeStruct((n, LANES), F32),
                   jax.ShapeDtypeStruct((N_EXPERTS, n), F32)],
        compiler_params=_params(("arbitrary",)),
        name="router",
    )(x, nw, sh, sc, rw)


def _select_kernel(a_ref, pose_ref, post_ref, off_ref, *, cap):
    n = a_ref.shape[1]
    nb = n // LANES
    def search(it, thr):
        bits = pltpu.bitcast(a_ref[...], I32)
        cand = thr | lax.shift_left(jnp.int32(1), 30 - it)
        cnt = jnp.sum(jnp.where(bits >= cand, 1.0, 0.0), axis=1, keepdims=True)
        return jnp.where(cnt >= cap, cand, thr)

    thr = lax.fori_loop(0, 31, search, jnp.zeros((N_EXPERTS, 1), I32))
    n_gt = jnp.sum(jnp.where(pltpu.bitcast(a_ref[...], I32) > thr, 1.0, 0.0), axis=1, keepdims=True)
    need = cap - n_gt

    tri = jnp.where(lax.broadcasted_iota(I32, (LANES, LANES), 0) <= lax.broadcasted_iota(I32, (LANES, LANES), 1),
                    1.0, 0.0).astype(BF16)

    def block(b, carry):
        c_eq, c_sel = carry
        start = pl.multiple_of(b * LANES, LANES)
        bb = pltpu.bitcast(a_ref[:, pl.ds(start, LANES)], I32)
        gt = jnp.where(bb > thr, 1.0, 0.0)
        eq = jnp.where(bb == thr, 1.0, 0.0)
        inc_eq = _dot(eq.astype(BF16), tri)
        rank_eq = c_eq + inc_eq - eq
        sel = gt + eq * jnp.where(rank_eq < need, 1.0, 0.0)
        inc_sel = _dot(sel.astype(BF16), tri)
        pos = jnp.where(sel > 0.0, c_sel + inc_sel - sel, -1.0)
        pose_ref[:, pl.ds(start, LANES)] = pos.astype(I32)
        full = jnp.concatenate([pos, jnp.full((LANES - N_EXPERTS, LANES), -1.0, F32)], axis=0)
        post_ref[pl.ds(start, LANES), :] = full.T
        off_ref[b] = jnp.broadcast_to(c_sel, (N_EXPERTS, LANES))
        return c_eq + inc_eq[:, LANES - 1:LANES], c_sel + inc_sel[:, LANES - 1:LANES]

    zero = jnp.zeros((N_EXPERTS, 1), F32)
    lax.fori_loop(0, nb, block, (zero, zero))


def _select(afft, cap):
    n = afft.shape[1]
    nb = n // LANES
    return pl.pallas_call(
        functools.partial(_select_kernel, cap=cap),
        out_shape=[jax.ShapeDtypeStruct((N_EXPERTS, n), I32), jax.ShapeDtypeStruct((n, LANES), F32),
                   jax.ShapeDtypeStruct((nb, N_EXPERTS, LANES), F32)],
        compiler_params=_params(None),
        name="select",
    )(afft)


def _gather_kernel(jt_ref, tt_ref, fl_ref, pos_ref, h_ref, o_ref, acc_scr, *, sb, kp):
    e = pl.program_id(0)
    k = pl.program_id(1)
    idx = e * kp + k
    j = jt_ref[idx]
    flags = fl_ref[idx]

    @pl.when((flags & 1) != 0)
    def _():
        tb = pos_ref.shape[2]
        kb = min(tb, 256)
        slot = lax.broadcasted_iota(I32, (sb, kb), 0) + j * sb
        rows = jnp.zeros((sb, D), F32)
        for t in range(tb // kb):
            onehot = jnp.where(pos_ref[0, :, t * kb:(t + 1) * kb] == slot, 1.0, 0.0).astype(BF16)
            rows = rows + _dot(onehot, h_ref[t * kb:(t + 1) * kb, :])

        @pl.when((flags & 2) != 0)
        def _():
            acc_scr[...] = rows

        @pl.when((flags & 2) == 0)
        def _():
            acc_scr[...] = acc_scr[...] + rows

    @pl.when((flags & 4) != 0)
    def _():
        o_ref[0] = acc_scr[...].astype(o_ref.dtype)


def _gather(jt, tt, fl, pose, h2, cap, sb, tb, kp):
    n = h2.shape[0]
    nt = n // tb
    pos3 = pose.reshape(N_EXPERTS * nt, 1, tb)
    grid_spec = pltpu.PrefetchScalarGridSpec(
        num_scalar_prefetch=3,
        grid=(N_EXPERTS, kp),
        in_specs=[
            pl.BlockSpec((1, 1, tb), lambda e, k, jt, tt, fl: (e * nt + tt[e * kp + k], 0, 0)),
            pl.BlockSpec((tb, D), lambda e, k, jt, tt, fl: (tt[e * kp + k], 0)),
        ],
        out_specs=pl.BlockSpec((1, sb, D), lambda e, k, jt, tt, fl: (e, jt[e * kp + k], 0)),
        scratch_shapes=[pltpu.VMEM((sb, D), F32)],
    )
    return pl.pallas_call(
        functools.partial(_gather_kernel, sb=sb, kp=kp),
        grid_spec=grid_spec,
        out_shape=jax.ShapeDtypeStruct((N_EXPERTS, cap, D), BF16),
        compiler_params=_params(("arbitrary", "arbitrary")),
        name="gather",
    )(jt, tt, fl, pos3, h2)


def _ffn_kernel(x_ref, w1_ref, w3_ref, w2_ref, o_ref, acc_scr):
    f = pl.program_id(1)
    nf = pl.num_programs(1)
    @pl.when(f == 0)
    def _():
        acc_scr[...] = jnp.zeros(acc_scr.shape, F32)

    w1 = w1_ref[0, 0].astype(BF16)
    w3 = w3_ref[0, 0].astype(BF16)
    w2 = w2_ref[0, 0].astype(BF16)
    cap = acc_scr.shape[0]
    rb = min(cap, 512)
    for r in range(cap // rb):
        x = x_ref[0, r * rb:(r + 1) * rb, :]
        he = (_silu(_dot(x, w1)) * _dot(x, w3)).astype(BF16)
        acc_scr[r * rb:(r + 1) * rb, :] = acc_scr[r * rb:(r + 1) * rb, :] + _dot(he, w2)

    @pl.when(f == nf - 1)
    def _():
        o_ref[0] = acc_scr[...].astype(o_ref.dtype)


def _ffn(xe, w1, w3, w2, l):
    cap = xe.shape[1]
    fc = 256
    return pl.pallas_call(
        _ffn_kernel,
        grid=(N_EXPERTS, EXPERT_FF // fc),
        in_specs=[
            pl.BlockSpec((1, cap, D), lambda e, f: (e, 0, 0)),
            pl.BlockSpec((1, 1, D, fc), lambda e, f: (l, e, 0, f)),
            pl.BlockSpec((1, 1, D, fc), lambda e, f: (l, e, 0, f)),
            pl.BlockSpec((1, 1, fc, D), lambda e, f: (l, e, f, 0)),
        ],
        out_specs=pl.BlockSpec((1, cap, D), lambda e, f: (e, 0, 0)),
        out_shape=jax.ShapeDtypeStruct((N_EXPERTS, cap, D), BF16),
        scratch_shapes=[pltpu.VMEM((cap, D), F32)],
        compiler_params=_params(("arbitrary", "arbitrary")),
        name="ffn",
    )(xe, w1, w3, w2)


def _combine_kernel(w0_ref, cnt_ref, pos_ref, gate_ref, x_ref, m5_ref, fw_ref, ye_hbm, o_ref, ybuf, xbuf, acc_scr, sem,
                    *, cap, win, final):
    b = pl.program_id(0)
    tb = pos_ref.shape[0]
    slot = lax.broadcasted_iota(I32, (tb, win), 1).astype(F32)

    def window(e, wb):
        lo = (w0_ref[b * N_EXPERTS + e] // BF16_ROWS) * BF16_ROWS + wb * win
        start = pl.multiple_of(jnp.minimum(lo, cap - win), BF16_ROWS)
        return lo, start

    def fetch(e, start, buf, s):
        return pltpu.make_async_copy(ye_hbm.at[e, pl.ds(start, win), :], buf, s)

    for e in range(N_EXPERTS):
        fetch(e, window(e, 0)[1], ybuf.at[e], sem.at[e]).start()

    acc = jnp.zeros((tb, D), F32)
    for e in range(N_EXPERTS):
        start = window(e, 0)[1]
        fetch(e, start, ybuf.at[e], sem.at[e]).wait()
        onehot = jnp.where(pos_ref[:, e:e + 1] - start.astype(F32) == slot, 1.0, 0.0).astype(BF16)
        acc = acc + _dot(onehot, ybuf[e]) * gate_ref[:, e:e + 1]
    acc_scr[...] = acc

    for e in range(N_EXPERTS):
        w0 = w0_ref[b * N_EXPERTS + e]
        n_win = (w0 % BF16_ROWS + cnt_ref[b * N_EXPERTS + e] + win - 1) // win

        def extra(wb, carry, e=e):
            lo, start = window(e, wb)
            cp = fetch(e, start, xbuf, sem.at[N_EXPERTS])
            cp.start()
            cp.wait()
            posc = pos_ref[:, e:e + 1]
            hit = jnp.where(posc - start.astype(F32) == slot, 1.0, 0.0) * jnp.where(posc >= lo.astype(F32), 1.0, 0.0)
            acc_scr[...] = acc_scr[...] + _dot(hit.astype(BF16), xbuf[...]) * gate_ref[:, e:e + 1]
            return carry

        lax.fori_loop(1, n_win, extra, 0)

    y = x_ref[...] + m5_ref[...] * acc_scr[...]
    if final:
        y = y * lax.rsqrt(jnp.mean(y * y, axis=-1, keepdims=True) + EPS) * fw_ref[...]
    o_ref[...] = y


def _combine(w0, cnt, post, aff, ye, x, m5, fw, tb, final):
    n = x.shape[0]
    cap = ye.shape[1]
    win = min(COMBINE_WIN, cap)
    vec = pl.BlockSpec((1, D), lambda b, w0, cnt: (0, 0))
    grid_spec = pltpu.PrefetchScalarGridSpec(
        num_scalar_prefetch=2,
        grid=(n // tb,),
        in_specs=[
            pl.BlockSpec((tb, LANES), lambda b, w0, cnt: (b, 0)),
            pl.BlockSpec((tb, LANES), lambda b, w0, cnt: (b, 0)),
            pl.BlockSpec((tb, D), lambda b, w0, cnt: (b, 0)),
            vec, vec,
            pl.BlockSpec(memory_space=pl.ANY),
        ],
        out_specs=pl.BlockSpec((tb, D), lambda b, w0, cnt: (b, 0)),
        scratch_shapes=[pltpu.VMEM((N_EXPERTS, win, D), BF16), pltpu.VMEM((win, D), BF16), pltpu.VMEM((tb, D), F32),
                        pltpu.SemaphoreType.DMA((N_EXPERTS + 1,))],
    )
    return pl.pallas_call(
        functools.partial(_combine_kernel, cap=cap, win=win, final=final),
        grid_spec=grid_spec,
        out_shape=jax.ShapeDtypeStruct((n, D), F32),
        compiler_params=_params(("arbitrary",)),
        name="combine",
    )(w0, cnt, post, aff, x, m5, fw, ye)


def _pair_tables(off_tb, cap, sb, kp):
    ne, nt = off_tb.shape
    nxt = jnp.concatenate([off_tb[:, 1:], jnp.full((ne, 1), cap, I32)], axis=1)
    cnt = nxt - off_tb
    j_lo = off_tb // sb
    j_hi = jnp.where(cnt > 0, (nxt - 1) // sb, j_lo - 1)
    npair = j_hi - j_lo + 1
    starts = jnp.cumsum(npair, axis=1) - npair
    total = jnp.sum(npair, axis=1, keepdims=True)
    k = jnp.arange(kp, dtype=I32)[None, :]
    kk = jnp.minimum(k, total - 1)
    tb_k = jnp.sum((starts[:, None, :] <= kk[:, :, None]).astype(I32), axis=2) - 1
    j_k = jnp.take_along_axis(j_lo, tb_k, axis=1) + kk - jnp.take_along_axis(starts, tb_k, axis=1)
    valid = (k < total).astype(I32)
    prev_j = jnp.concatenate([jnp.full((ne, 1), -1, I32), j_k[:, :-1]], axis=1)
    first = ((j_k != prev_j) & (k < total)).astype(I32)
    next_j = jnp.concatenate([j_k[:, 1:], jnp.full((ne, 1), -1, I32)], axis=1)
    last = (((j_k != next_j) | (k + 1 >= total)) & (k < total)).astype(I32)
    flags = valid + 2 * first + 4 * last
    return j_k.reshape(-1).astype(I32), tb_k.reshape(-1).astype(I32), flags.reshape(-1).astype(I32)


def _moe(x, nw, sh, sc, m5, rw, w1, w3, w2, l, fw, final):
    n = x.shape[0]
    cap = CAPACITY_FACTOR * n // N_EXPERTS
    sb = min(256, cap)
    tbg = min(1024, n)
    tbc = min(256, n)
    kp = cap // sb + n // tbg
    h2, aff, afft = _router(x, nw, sh, sc, rw)
    pose, post, off = _select(afft, cap)
    off128 = off[:, :, 0].T.astype(I32)
    jt, tt, fl = _pair_tables(off128[:, :: tbg // LANES], cap, sb, kp)
    xe = _gather(jt, tt, fl, pose, h2, cap, sb, tbg, kp)
    ye = _ffn(xe, w1, w3, w2, l)
    off_c = off128[:, :: tbc // LANES]
    cnt_c = jnp.concatenate([off_c[:, 1:], jnp.full((N_EXPERTS, 1), cap, I32)], axis=1) - off_c
    return _combine(off_c.T.reshape(-1), cnt_c.T.reshape(-1), post, aff, ye, x, m5, fw, tbc, final)


def _rope_tables(n_rows):
    row = jnp.repeat(jnp.arange(n_rows, dtype=F32), GRID_W)
    col = jnp.tile(jnp.arange(GRID_W, dtype=F32), n_rows)
    half = DIFF_HEAD_DIM // 2
    freqs = ROPE_BASE ** (-jnp.arange(0, half, 2, dtype=F32) / half)
    ar, ac = row[:, None] * freqs, col[:, None] * freqs
    cr, sr, cc, sn = jnp.cos(ar), jnp.sin(ar), jnp.cos(ac), jnp.sin(ac)
    z = jnp.zeros_like(sr)
    cos = jnp.concatenate([cr, cr, cc, cc], axis=1)
    sa = jnp.concatenate([-sr, z, -sn, z], axis=1)
    sb = jnp.concatenate([z, sr, z, sn], axis=1)
    return tuple(jnp.tile(t, (1, 2)) for t in (cos, sa, sb))


def _identity_tables(n):
    return (jnp.ones((n, LANES), F32), jnp.zeros((n, LANES), F32), jnp.zeros((n, LANES), F32))


def _proj_weight(w_in):
    sizes = (1024, 256, 32, 1024, 1024, 256, 1024, 1024, 2048, 3072)
    idx = np.cumsum((0,) + sizes)
    px, pb, pdt, pk, pv, pc, pq, pz, pglu, pgate = [w_in[:, idx[i]:idx[i + 1]] for i in range(10)]
    pad = jnp.zeros((D, LANES - SSD_HEADS), w_in.dtype)
    tail = jnp.zeros((D, N_PROJ - OFF_DT - 2 * LANES), w_in.dtype)
    w = jnp.concatenate([px, pz, pk, pv, pq, pglu, pgate, pb, pc,
                         pdt[:, :SSD_HEADS], pad, pdt[:, SSD_HEADS:], pad, tail], axis=1)
    return w.astype(BF16)


def _lane_pad(v):
    return jnp.pad(v, ((0, 0), (0, LANES - v.shape[1]))).reshape(2, 1, LANES)


def _mixer_front(x, nw, sh, sc, w, conv_w, conv_b):
    p = _inproj(x, nw, sh, sc, w)
    xbc = _ssdconv(p, conv_w, conv_b)
    return p, xbc


def kernel(x, c, ctx, c_ctx, ada_w, ada_b, norm1_w, norm2_w, w_in, ssd_conv_w, ssd_conv_b, ssd_dt_bias, ssd_a_log, ssd_d, ssd_norm_w, ssd_out, diff_lambda, diff_subln_w, diff_out, conf_dw_w, conf_dw_b, conf_ln_w, conf_ln_b, conf_out, w_o, router_w, exp_w1, exp_w3, exp_w2, final_norm_w):
    depth = ada_w.shape[0]
    n = x.shape[1]
    nctx = ctx.shape[1]
    x_lat = x[0]
    x_ctx = ctx[0]

    cc = jnp.zeros((8, D), F32).at[0].set(c[0]).at[1].set(c_ctx)
    mods = _ada(cc, ada_w, ada_b)

    rope = _rope_tables(n // GRID_W)
    ident_ctx = _identity_tables(nctx)
    scale = DIFF_HEAD_DIM ** -0.5 * math.log2(math.e)
    zero_state = jnp.zeros((2, SSD_STATE, SSD_HEADS * SSD_HEAD_DIM), F32)
    fw = final_norm_w.reshape(1, D)

    for l in range(depth):
        last = l == depth - 1
        lam_init = 0.8 - 0.6 * math.exp(-0.3 * l)
        lq1, lk1, lq2, lk2 = diff_lambda[l].astype(F32)
        lam = (jnp.exp(jnp.sum(lq1 * lk1)) - jnp.exp(jnp.sum(lq2 * lk2)) + lam_init).reshape(1)
        m_lat = [mods[l, 0:1, i * D:(i + 1) * D] for i in range(N_MOD)]
        m_ctx = [mods[l, 1:2, i * D:(i + 1) * D] for i in range(N_MOD)]
        nw1 = norm1_w[l].reshape(1, D)
        nw2 = norm2_w[l].reshape(1, D)
        w = _proj_weight(w_in[l])
        dtb = _lane_pad(ssd_dt_bias[l])
        alog = _lane_pad(ssd_a_log[l])
        dsk = jnp.repeat(ssd_d[l], SSD_HEAD_DIM).reshape(1, D)
        gnw = ssd_norm_w[l].reshape(1, D)
        sw = diff_subln_w[l].reshape(1, LANES)
        wa, wb, wc, wo = (t[l].astype(BF16) for t in (ssd_out, diff_out, conf_out, w_o))
        rw = jnp.pad(router_w[l], ((0, 0), (0, LANES - N_EXPERTS))).astype(BF16)
        coef = 1.0 - lam_init

        p_c, xbc_c = _mixer_front(x_ctx, nw1, m_ctx[0], m_ctx[1], w, ssd_conv_w[l], ssd_conv_b[l])
        yf_c, yb_c, st_c = _ssd(xbc_c, p_c, dtb, alog, zero_state)
        p_l, xbc_l = _mixer_front(x_lat, nw1, m_lat[0], m_lat[1], w, ssd_conv_w[l], ssd_conv_b[l])
        k_all, v_all = _kvprep(p_c, p_l, rope)

        if not last:
            q_c = _qprep(p_c, ident_ctx, scale)
            on_c = _attention(lam, q_c, k_all, v_all, sw, nctx, coef)
            cn_c = _conformer(p_c, conf_dw_w[l], conf_dw_b[l], conf_ln_w[l], conf_ln_b[l])
            x_ctx = _merge(yf_c, yb_c, xbc_c, p_c, on_c, cn_c, x_ctx, dsk, gnw, m_ctx[2], wa, wb, wc, wo)
            x_ctx = _moe(x_ctx, nw2, m_ctx[3], m_ctx[4], m_ctx[5], rw, exp_w1, exp_w3, exp_w2, l, fw, False)

        yf_l, yb_l, _ = _ssd(xbc_l, p_l, dtb, alog, st_c)
        q_l = _qprep(p_l, rope, scale)
        on_l = _attention(lam, q_l, k_all, v_all, sw, nctx + n, coef)
        cn_l = _conformer(p_l, conf_dw_w[l], conf_dw_b[l], conf_ln_w[l], conf_ln_b[l])
        x_lat = _merge(yf_l, yb_l, xbc_l, p_l, on_l, cn_l, x_lat, dsk, gnw, m_lat[2], wa, wb, wc, wo)
        x_lat = _moe(x_lat, nw2, m_lat[3], m_lat[4], m_lat[5], rw, exp_w1, exp_w3, exp_w2, l, fw, last)

    return x_lat[None]
```

```python
import functools
import math

import jax
import jax.numpy as jnp
import numpy as np
from jax import lax
from jax.experimental import pallas as pl
from jax.experimental.pallas import tpu as pltpu

F32 = jnp.float32
BF16 = jnp.bfloat16
I32 = jnp.int32

EPS = 1e-6
D = 1024
N_MOD = 6
GRID_W = 64
SSD_HEADS = 16
SSD_HEAD_DIM = 64
SSD_STATE = 128
SSD_GN = 256
SSD_CHUNK = 128
DIFF_HEADS = 8
DIFF_HEAD_DIM = 64
ROPE_BASE = 10000.0
CONF_KERNEL = 31
CONF_HALO = 16
N_EXPERTS = 16
EXPERT_FF = 2048
CAPACITY_FACTOR = 2
LANES = 128
BF16_ROWS = 16
COMBINE_WIN = 128

OFF_X, OFF_Z, OFF_GLU_A, OFF_GLU_G, OFF_GATE = 0, 1024, 2048, 3072, 4096
OFF_B, OFF_C, OFF_DT = 7168, 7424, 7680
N_PROJ_F32 = 8192
OFF_K, OFF_V, OFF_Q = 8192, 9216, 10240
N_PROJ = 11264

VMEM_LIMIT = 56 * 1024 * 1024


def _params(sem, vmem=VMEM_LIMIT):
    return pltpu.CompilerParams(dimension_semantics=sem, vmem_limit_bytes=vmem)


def _silu(x):
    return x * jax.nn.sigmoid(x)


def _split3(x):
    a1 = x.astype(BF16)
    r1 = x - a1.astype(F32)
    a2 = r1.astype(BF16)
    a3 = (r1 - a2.astype(F32)).astype(BF16)
    return a1, a2, a3


def _dot(a, b):
    return jnp.dot(a, b, preferred_element_type=F32)


def _ada_kernel(c_ref, w_ref, b_ref, o_ref):
    a = _silu(c_ref[...]).astype(BF16)
    o_ref[0] = _dot(a, w_ref[0].astype(BF16)) + b_ref[0]


def _ada(cc, ada_w, ada_b):
    nl = ada_w.shape[0]
    tn = 1536
    return pl.pallas_call(
        _ada_kernel,
        grid=(nl, N_MOD * D // tn),
        in_specs=[
            pl.BlockSpec((8, D), lambda l, j: (0, 0)),
            pl.BlockSpec((1, D, tn), lambda l, j: (l, 0, j)),
            pl.BlockSpec((1, 1, tn), lambda l, j: (l, 0, j)),
        ],
        out_specs=pl.BlockSpec((1, 8, tn), lambda l, j: (l, 0, j)),
        out_shape=jax.ShapeDtypeStruct((nl, 8, N_MOD * D), F32),
        compiler_params=_params(("arbitrary", "arbitrary")),
        name="ada",
    )(cc, ada_w, ada_b.reshape(nl, 1, N_MOD * D))


def _modulate(x, nw, sh, sc):
    r = lax.rsqrt(jnp.mean(x * x, axis=-1, keepdims=True) + EPS)
    return (x * r * nw) * (1.0 + sc) + sh


def _rope(y, cos, sa, sb):
    parts = []
    for j in range(D // LANES):
        yj = y[:, j * LANES:(j + 1) * LANES]
        parts.append(yj * cos + pltpu.roll(yj, LANES - 16, 1) * sa + pltpu.roll(yj, 16, 1) * sb)
    return jnp.concatenate(parts, axis=1)


def _inproj_kernel(x_ref, nw_ref, sh_ref, sc_ref, w_ref, cos_ref, sa_ref, sb_ref, k_in, v_in,
                   p_ref, k_ref, v_ref, q_ref, h_scr, *, scale):
    del k_in, v_in
    j = pl.program_id(1)
    nf = N_PROJ_F32 // D

    @pl.when(j == 0)
    def _():
        h_scr[...] = _modulate(x_ref[...], nw_ref[...], sh_ref[...], sc_ref[...]).astype(BF16)

    y = _dot(h_scr[...], w_ref[...])

    @pl.when(j < nf)
    def _():
        p_ref[...] = y

    @pl.when(j == nf)
    def _():
        k_ref[...] = _rope(y, cos_ref[...], sa_ref[...], sb_ref[...]).astype(BF16)

    @pl.when(j == nf + 1)
    def _():
        v_ref[...] = y.astype(BF16)

    @pl.when(j == nf + 2)
    def _():
        q_ref[...] = (_rope(y, cos_ref[...], sa_ref[...], sb_ref[...]) * scale).astype(BF16)


def _inproj(x, nw, sh, sc, w, tabs, scale, k_all, v_all, row_block):
    n = x.shape[0]
    tm = min(n, 1024)
    tn = D
    nf = N_PROJ_F32 // tn
    vec = pl.BlockSpec((1, D), lambda i, j: (0, 0))
    tab = pl.BlockSpec((tm, LANES), lambda i, j: (i, 0))
    kv_spec = pl.BlockSpec((tm, D), lambda i, j: (row_block + i, 0))
    kv_shape = jax.ShapeDtypeStruct(k_all.shape, BF16)
    hbm = pl.BlockSpec(memory_space=pl.ANY)
    return pl.pallas_call(
        functools.partial(_inproj_kernel, scale=scale),
        grid=(n // tm, N_PROJ // tn),
        in_specs=[pl.BlockSpec((tm, D), lambda i, j: (i, 0)), vec, vec, vec,
                  pl.BlockSpec((D, tn), lambda i, j: (0, j)), tab, tab, tab, hbm, hbm],
        out_specs=[pl.BlockSpec((tm, tn), lambda i, j: (i, jnp.minimum(j, nf - 1))), kv_spec, kv_spec,
                   pl.BlockSpec((tm, D), lambda i, j: (i, 0))],
        out_shape=[jax.ShapeDtypeStruct((n, N_PROJ_F32), F32), kv_shape, kv_shape,
                   jax.ShapeDtypeStruct((n, D), BF16)],
        scratch_shapes=[pltpu.VMEM((tm, D), BF16)],
        input_output_aliases={8: 1, 9: 2},
        compiler_params=_params(("arbitrary", "arbitrary")),
        name="inproj",
    )(x, nw, sh, sc, w, *tabs, k_all, v_all)


def _ssdconv_kernel(prev_ref, cur_ref, next_ref, w_ref, b_ref, o_ref):
    i = pl.program_id(0)
    nt = pl.num_programs(0)
    u = cur_ref[...]
    t = u.shape[0]
    row = lax.broadcasted_iota(I32, u.shape, 0)
    before = jnp.where(i > 0, prev_ref[7:8, :], 0.0)
    after = jnp.where(i < nt - 1, next_ref[0:1, :], 0.0)
    um1 = jnp.where(row == 0, before, pltpu.roll(u, 1, 0))
    up1 = jnp.where(row == t - 1, after, pltpu.roll(u, t - 1, 0))
    y = um1 * w_ref[0:1, :] + u * w_ref[1:2, :] + up1 * w_ref[2:3, :] + b_ref[...]
    o_ref[...] = _silu(y)


def _ssdconv(p, conv_w, conv_b):
    n = p.shape[0]
    t = min(n, 512)
    cw = 512
    r8 = t // 8
    nb8 = n // 8

    def col(c):
        return jnp.where(c < 2, c, OFF_B // cw)

    return pl.pallas_call(
        _ssdconv_kernel,
        grid=(n // t, 3),
        in_specs=[
            pl.BlockSpec((8, cw), lambda i, c: (jnp.maximum(i * r8 - 1, 0), col(c))),
            pl.BlockSpec((t, cw), lambda i, c: (i, col(c))),
            pl.BlockSpec((8, cw), lambda i, c: (jnp.minimum((i + 1) * r8, nb8 - 1), col(c))),
            pl.BlockSpec((3, cw), lambda i, c: (0, c)),
            pl.BlockSpec((1, cw), lambda i, c: (0, c)),
        ],
        out_specs=pl.BlockSpec((t, cw), lambda i, c: (i, c)),
        out_shape=jax.ShapeDtypeStruct((n, 3 * cw), F32),
        compiler_params=_params(("arbitrary", "arbitrary")),
        name="ssdconv",
    )(p, p, p, conv_w, conv_b.reshape(1, -1))


def _ssd_direction(d, xs_ref, b_ref, c_ref, dt_ref, dtb, alog, st_ref, y_ref):
    L = SSD_CHUNK
    P = SSD_HEAD_DIM
    HP = SSD_HEADS * P
    GW = HP // 2

    x = dt_ref[...] + dtb
    dt = jnp.maximum(x, 0.0) + jnp.log1p(jnp.exp(-jnp.abs(x)))
    a = dt * (-jnp.exp(alog))

    li = lax.broadcasted_iota(I32, (L, L), 0)
    si = lax.broadcasted_iota(I32, (L, L), 1)
    keep = si <= li if d == 0 else si >= li
    tri = jnp.where(keep, 1.0, 0.0).astype(BF16)
    a1, a2, a3 = _split3(a)
    cum = _dot(tri, a1) + _dot(tri, a2) + _dot(tri, a3)
    cum_t = cum.T

    ex = jnp.where(lax.broadcasted_iota(I32, (LANES, HP), 0) == lax.broadcasted_iota(I32, (LANES, HP), 1) // P,
                   1.0, 0.0).astype(BF16)

    def expand(v):
        v1, v2, v3 = _split3(v)
        return _dot(v1, ex) + _dot(v2, ex) + _dot(v3, ex)

    cum_e = expand(cum)
    dt_e = expand(dt)
    tot_e = cum_e[L - 1:L, :] if d == 0 else cum_e[0:1, :]

    xdt = xs_ref[...] * dt_e
    xb = xdt.astype(BF16)
    xw = (xdt * jnp.exp(tot_e - cum_e)).astype(BF16)
    bb = b_ref[...].astype(BF16)
    cb = c_ref[...].astype(BF16)
    st = st_ref[...]
    stb = st.astype(BF16)
    grow = jnp.exp(cum_e)

    for g in range(2):
        bg = bb[:, g * SSD_STATE:(g + 1) * SSD_STATE]
        cg = cb[:, g * SSD_STATE:(g + 1) * SSD_STATE]
        scores = lax.dot_general(cg, bg, (((1,), (1,)), ((), ())), preferred_element_type=F32)
        y_off = _dot(cg, stb[:, g * GW:(g + 1) * GW]) * grow[:, g * GW:(g + 1) * GW]
        for hh in range(SSD_HEADS // 2):
            h = g * (SSD_HEADS // 2) + hh
            seg = cum[:, h:h + 1] - cum_t[h:h + 1, :]
            dec = jnp.where(keep, jnp.exp(jnp.where(keep, seg, 0.0)), 0.0)
            m = (scores * dec).astype(BF16)
            yd = _dot(m, xb[:, h * P:(h + 1) * P])
            y_ref[:, h * P:(h + 1) * P] = yd + y_off[:, hh * P:(hh + 1) * P]
        upd = lax.dot_general(bg, xw[:, g * GW:(g + 1) * GW], (((0,), (0,)), ((), ())),
                              preferred_element_type=F32)
        st_ref[:, g * GW:(g + 1) * GW] = st[:, g * GW:(g + 1) * GW] * jnp.exp(tot_e[:, g * GW:(g + 1) * GW]) + upd


def _ssd_kernel(xsf_ref, bf_ref, cf_ref, dtf_ref, xsb_ref, bb_ref, cb_ref, dtb_ref, bias_ref, alog_ref, h0_ref,
                yf_ref, yb_ref, hT_ref, st_scr):
    c = pl.program_id(0)

    @pl.when(c == 0)
    def _():
        st_scr[...] = h0_ref[...]

    L = SSD_CHUNK
    nsub = xsf_ref.shape[0] // L
    for k in range(nsub):
        f = slice(k * L, (k + 1) * L)
        r = slice((nsub - 1 - k) * L, (nsub - k) * L)
        _ssd_direction(0, xsf_ref.at[f], bf_ref.at[f], cf_ref.at[f], dtf_ref.at[f], bias_ref[0], alog_ref[0],
                       st_scr.at[0], yf_ref.at[f])
        _ssd_direction(1, xsb_ref.at[r], bb_ref.at[r], cb_ref.at[r], dtb_ref.at[r], bias_ref[1], alog_ref[1],
                       st_scr.at[1], yb_ref.at[r])

    @pl.when(c == pl.num_programs(0) - 1)
    def _():
        hT_ref[...] = st_scr[...]


def _ssd(xbc, p, dtb, alog, h0):
    n = xbc.shape[0]
    L = SSD_CHUNK * (2 if n % (2 * SSD_CHUNK) == 0 else 1)
    nc = n // L
    HP = SSD_HEADS * SSD_HEAD_DIM
    full = pl.BlockSpec((2, SSD_STATE, HP), lambda c: (0, 0, 0))
    vec = pl.BlockSpec((2, 1, LANES), lambda c: (0, 0, 0))

    def side(chunk, d):
        return [
            pl.BlockSpec((L, HP), lambda c: (chunk(c), 0)),
            pl.BlockSpec((L, SSD_GN), lambda c: (chunk(c), HP // SSD_GN)),
            pl.BlockSpec((L, SSD_GN), lambda c: (chunk(c), HP // SSD_GN + 1)),
            pl.BlockSpec((L, LANES), lambda c: (chunk(c), OFF_DT // LANES + d)),
        ]

    def fwd(c):
        return c

    def bwd(c):
        return nc - 1 - c

    return pl.pallas_call(
        _ssd_kernel,
        grid=(nc,),
        in_specs=side(fwd, 0) + side(bwd, 1) + [vec, vec, full],
        out_specs=[pl.BlockSpec((L, HP), lambda c: (c, 0)), pl.BlockSpec((L, HP), lambda c: (nc - 1 - c, 0)), full],
        out_shape=[jax.ShapeDtypeStruct((n, HP), F32), jax.ShapeDtypeStruct((n, HP), F32),
                   jax.ShapeDtypeStruct((2, SSD_STATE, HP), F32)],
        scratch_shapes=[pltpu.VMEM((2, SSD_STATE, HP), F32)],
        compiler_params=_params(("arbitrary",)),
        name="ssd",
    )(xbc, xbc, xbc, p, xbc, xbc, xbc, p, dtb, alog, h0)


def _attn_kernel(lam_ref, q_ref, k_ref, v_ref, sw_ref, o_ref, q2_scr, sa_scr, sb_scr, p_scr, m_scr, acc_scr,
                 *, tq, nq, ck, nchunks, coef):
    lane = lax.broadcasted_iota(I32, (tq, LANES), 1)
    for t in range(nq):
        q = q_ref[t * tq:(t + 1) * tq, :]
        zero = jnp.zeros_like(q)
        q2_scr[t, 0:tq, :] = jnp.where(lane < DIFF_HEAD_DIM, q, zero)
        q2_scr[t, tq:2 * tq, :] = jnp.where(lane >= DIFF_HEAD_DIM, q, zero)
    m_scr[...] = jnp.full(m_scr.shape, -jnp.inf, F32)
    acc_scr[...] = jnp.zeros(acc_scr.shape, F32)
    ones = jnp.ones((ck, LANES), BF16)
    nb = ck // LANES

    def key_rows(c):
        return pl.ds(c * ck, ck) if isinstance(c, int) else pl.ds(pl.multiple_of(c * ck, ck), ck)

    def scores(t, c, s_scr):
        s_scr[...] = lax.dot_general(q2_scr[t], k_ref[key_rows(c), :], (((1,), (1,)), ((), ())),
                                     preferred_element_type=F32)

    def soft_pv(t, c, s_scr):
        rows = key_rows(c)
        mx = s_scr[:, 0:LANES]
        for b in range(1, nb):
            mx = jnp.maximum(mx, s_scr[:, b * LANES:(b + 1) * LANES])
        m_prev = m_scr[t]
        m_new = jnp.maximum(m_prev, jnp.max(mx, axis=1, keepdims=True))
        alpha = jnp.exp2(m_prev - m_new)
        for b in range(nb):
            p_scr[:, b * LANES:(b + 1) * LANES] = jnp.exp2(s_scr[:, b * LANES:(b + 1) * LANES] - m_new).astype(BF16)
        va = jnp.concatenate([v_ref[rows, :], ones], axis=1)
        acc_scr[t] = acc_scr[t] * jnp.concatenate([alpha, alpha], axis=1) + _dot(p_scr[...], va)
        m_scr[t] = m_new

    def tile(t, cur, nxt):
        def pair(jj, carry):
            scores(t, 2 * jj + 1, nxt)
            soft_pv(t, 2 * jj, cur)
            scores(t, 2 * jj + 2, cur)
            soft_pv(t, 2 * jj + 1, nxt)
            return carry

        lax.fori_loop(0, (nchunks - 1) // 2, pair, 0)
        last, free = cur, nxt
        if (nchunks - 1) % 2 == 1:
            scores(t, nchunks - 1, nxt)
            soft_pv(t, nchunks - 2, cur)
            last, free = nxt, cur
        if t + 1 < nq:
            scores(t + 1, 0, free)
        soft_pv(t, nchunks - 1, last)
        return free, last

    scores(0, 0, sa_scr)
    bufs = (sa_scr, sb_scr)
    for t in range(nq):
        bufs = tile(t, *bufs)

    for t in range(nq):
        on = acc_scr[t, :, 0:LANES] / acc_scr[t, :, LANES:2 * LANES]
        o = on[0:tq, :] - lam_ref[0] * on[tq:2 * tq, :]
        r = lax.rsqrt(jnp.mean(o * o, axis=-1, keepdims=True) + EPS)
        o_ref[t * tq:(t + 1) * tq, :] = ((o * r * sw_ref[...]) * coef).astype(o_ref.dtype)


def _pick_tk(m):
    for cand in (1280, 1024, 512, 256):
        if m % cand == 0:
            return cand
    return m


def _attention(lam, q, k, v, subln_w, m, key_block, coef):
    n = q.shape[0]
    tq = min(n, 512)
    nq = min(2, n // tq)
    ck = _pick_tk(m)
    return pl.pallas_call(
        functools.partial(_attn_kernel, tq=tq, nq=nq, ck=ck, nchunks=m // ck, coef=coef),
        grid=(DIFF_HEADS, n // (nq * tq)),
        in_specs=[
            pl.BlockSpec(memory_space=pltpu.SMEM),
            pl.BlockSpec((nq * tq, LANES), lambda h, i: (i, h)),
            pl.BlockSpec((m, LANES), lambda h, i: (key_block, h)),
            pl.BlockSpec((m, LANES), lambda h, i: (key_block, h)),
            pl.BlockSpec((1, LANES), lambda h, i: (0, 0)),
        ],
        out_specs=pl.BlockSpec((nq * tq, LANES), lambda h, i: (i, h)),
        out_shape=jax.ShapeDtypeStruct((n, D), BF16),
        scratch_shapes=[pltpu.VMEM((nq, 2 * tq, LANES), BF16), pltpu.VMEM((2 * tq, ck), F32),
                        pltpu.VMEM((2 * tq, ck), F32), pltpu.VMEM((2 * tq, ck), BF16),
                        pltpu.VMEM((nq, 2 * tq, LANES), F32), pltpu.VMEM((nq, 2 * tq, 2 * LANES), F32)],
        compiler_params=_params(("arbitrary", "arbitrary")),
        name="attn",
    )(lam, q, k, v, subln_w)


def _conf_kernel(ap_ref, gp_ref, a_ref, g_ref, an_ref, gn_ref, w_ref, b_ref, lw_ref, lb_ref, o_ref, ext_scr, sh_scr):
    i = pl.program_id(0)
    nt = pl.num_programs(0)
    t = a_ref.shape[0]
    hl = CONF_HALO

    def glu(a, g):
        return a[...] * jax.nn.sigmoid(g[...])

    ext_scr[0:hl, :] = jnp.where(i > 0, glu(ap_ref, gp_ref), 0.0)
    ext_scr[hl:hl + t, :] = glu(a_ref, g_ref)
    ext_scr[hl + t:2 * hl + t, :] = jnp.where(i < nt - 1, glu(an_ref, gn_ref), 0.0)

    span = t + 2 * hl - 8
    for r in range(1, 8):
        sh_scr[r, 0:span, :] = ext_scr[pl.ds(r, span), :]
    acc = jnp.zeros((t, D), F32) + b_ref[...]
    for k in range(CONF_KERNEL):
        off = k + hl - CONF_KERNEL // 2
        src = ext_scr if off % 8 == 0 else sh_scr.at[off % 8]
        acc = acc + src[pl.ds(off - off % 8, t), :] * w_ref[k:k + 1, :]
    mu = jnp.mean(acc, axis=-1, keepdims=True)
    xc = acc - mu
    y = xc * lax.rsqrt(jnp.mean(xc * xc, axis=-1, keepdims=True) + EPS) * lw_ref[...] + lb_ref[...]
    o_ref[...] = _silu(y).astype(o_ref.dtype)


def _conformer(p, dw_w, dw_b, ln_w, ln_b):
    n = p.shape[0]
    t = min(n, 256)
    hl = CONF_HALO
    rh = t // hl
    nbh = n // hl
    ca, cg = OFF_GLU_A // D, OFF_GLU_G // D

    def prev(i):
        return jnp.maximum(i * rh - 1, 0)

    def nxt(i):
        return jnp.minimum((i + 1) * rh, nbh - 1)

    vec = pl.BlockSpec((1, D), lambda i: (0, 0))
    return pl.pallas_call(
        _conf_kernel,
        grid=(n // t,),
        in_specs=[
            pl.BlockSpec((hl, D), lambda i: (prev(i), ca)), pl.BlockSpec((hl, D), lambda i: (prev(i), cg)),
            pl.BlockSpec((t, D), lambda i: (i, ca)), pl.BlockSpec((t, D), lambda i: (i, cg)),
            pl.BlockSpec((hl, D), lambda i: (nxt(i), ca)), pl.BlockSpec((hl, D), lambda i: (nxt(i), cg)),
            pl.BlockSpec((CONF_KERNEL, D), lambda i: (0, 0)), vec, vec, vec,
        ],
        out_specs=pl.BlockSpec((t, D), lambda i: (i, 0)),
        out_shape=jax.ShapeDtypeStruct((n, D), BF16),
        scratch_shapes=[pltpu.VMEM((t + 2 * hl, D), F32), pltpu.VMEM((8, t + 2 * hl, D), F32)],
        compiler_params=_params(("arbitrary",)),
        name="conformer",
    )(p, p, p, p, p, p, dw_w, dw_b.reshape(1, D), ln_w.reshape(1, D), ln_b.reshape(1, D))


def _merge_kernel(yf_ref, yb_ref, xs_ref, z_ref, on_ref, cn_ref, g0_ref, g1_ref, g2_ref, x_ref, dsk_ref, nw_ref,
                  m2_ref, wa_ref, wb_ref, wc_ref, wo_ref, o_ref):
    y = yf_ref[...] + yb_ref[...] + xs_ref[...] * dsk_ref[...]
    yz = y * _silu(z_ref[...])
    half = D // 2
    parts = []
    for g in range(2):
        seg = yz[:, g * half:(g + 1) * half]
        r = lax.rsqrt(jnp.mean(seg * seg, axis=-1, keepdims=True) + EPS)
        parts.append(seg * r * nw_ref[:, g * half:(g + 1) * half])
    gn = jnp.concatenate(parts, axis=1).astype(BF16)
    y_a = _dot(gn, wa_ref[...])
    y_b = _dot(on_ref[...], wb_ref[...])
    y_c = _dot(cn_ref[...], wc_ref[...])
    m = jax.nn.sigmoid(g0_ref[...]) * y_a + jax.nn.sigmoid(g1_ref[...]) * y_b + jax.nn.sigmoid(g2_ref[...]) * y_c
    y_o = _dot(m.astype(BF16), wo_ref[...])
    o_ref[...] = x_ref[...] + m2_ref[...] * y_o


def _merge(yf, yb, xbc, p, on, cn, x, dsk, nw, m2, wa, wb, wc, wo):
    n = x.shape[0]
    tm = min(n, 256)
    vec = pl.BlockSpec((1, D), lambda i: (0, 0))
    wsp = pl.BlockSpec((D, D), lambda i: (0, 0))
    row = pl.BlockSpec((tm, D), lambda i: (i, 0))
    return pl.pallas_call(
        _merge_kernel,
        grid=(n // tm,),
        in_specs=[
            row, row,
            row,
            pl.BlockSpec((tm, D), lambda i: (i, OFF_Z // D)),
            row, row,
            pl.BlockSpec((tm, D), lambda i: (i, OFF_GATE // D)),
            pl.BlockSpec((tm, D), lambda i: (i, OFF_GATE // D + 1)),
            pl.BlockSpec((tm, D), lambda i: (i, OFF_GATE // D + 2)),
            row, vec, vec, vec, wsp, wsp, wsp, wsp,
        ],
        out_specs=row,
        out_shape=jax.ShapeDtypeStruct((n, D), F32),
        compiler_params=_params(("arbitrary",)),
        name="merge",
    )(yf, yb, xbc, p, on, cn, p, p, p, x, dsk, nw, m2, wa, wb, wc, wo)


def _router_kernel(x_ref, nw_ref, sh_ref, sc_ref, rw_ref, h_ref, aff_ref, afft_ref):
    h = _modulate(x_ref[...], nw_ref[...], sh_ref[...], sc_ref[...]).astype(BF16)
    h_ref[...] = h
    logits = _dot(h, rw_ref[...])
    lane = lax.broadcasted_iota(I32, logits.shape, 1)
    logits = jnp.where(lane < N_EXPERTS, logits, -jnp.inf)
    e = jnp.exp(logits - jnp.max(logits, axis=-1, keepdims=True))
    aff = e / jnp.sum(e, axis=-1, keepdims=True)
    aff_ref[...] = aff
    afft_ref[...] = aff.T[0:N_EXPERTS, :]


def _router(x, nw, sh, sc, rw):
    n = x.shape[0]
    tm = min(n, 512)
    vec = pl.BlockSpec((1, D), lambda i: (0, 0))
    return pl.pallas_call(
        _router_kernel,
        grid=(n // tm,),
        in_specs=[pl.BlockSpec((tm, D), lambda i: (i, 0)), vec, vec, vec, pl.BlockSpec((D, LANES), lambda i: (0, 0))],
        out_specs=[pl.BlockSpec((tm, D), lambda i: (i, 0)), pl.BlockSpec((tm, LANES), lambda i: (i, 0)),
                   pl.BlockSpec((N_EXPERTS, tm), lambda i: (0, i))],
        out_shape=[jax.ShapeDtypeStruct((n, D), BF16), jax.ShapeDtypeStruct((n, LANES), F32),
                   jax.ShapeDtypeStruct((N_EXPERTS, n), F32)],
        compiler_params=_params(("arbitrary",)),
        name="router",
    )(x, nw, sh, sc, rw)


def _select_kernel(a_ref, pose_ref, post_ref, off_ref, *, cap):
    n = a_ref.shape[1]
    nb = n // LANES
    def search(it, thr):
        bits = pltpu.bitcast(a_ref[...], I32)
        cand = thr | lax.shift_left(jnp.int32(1), 30 - it)
        cnt = jnp.sum(jnp.where(bits >= cand, 1.0, 0.0), axis=1, keepdims=True)
        return jnp.where(cnt >= cap, cand, thr)

    thr = lax.fori_loop(0, 31, search, jnp.zeros((N_EXPERTS, 1), I32))
    n_gt = jnp.sum(jnp.where(pltpu.bitcast(a_ref[...], I32) > thr, 1.0, 0.0), axis=1, keepdims=True)
    need = cap - n_gt

    tri = jnp.where(lax.broadcasted_iota(I32, (LANES, LANES), 0) <= lax.broadcasted_iota(I32, (LANES, LANES), 1),
                    1.0, 0.0).astype(BF16)

    def block(b, carry):
        c_eq, c_sel = carry
        start = pl.multiple_of(b * LANES, LANES)
        bb = pltpu.bitcast(a_ref[:, pl.ds(start, LANES)], I32)
        gt = jnp.where(bb > thr, 1.0, 0.0)
        eq = jnp.where(bb == thr, 1.0, 0.0)
        inc_eq = _dot(eq.astype(BF16), tri)
        rank_eq = c_eq + inc_eq - eq
        sel = gt + eq * jnp.where(rank_eq < need, 1.0, 0.0)
        inc_sel = _dot(sel.astype(BF16), tri)
        pos = jnp.where(sel > 0.0, c_sel + inc_sel - sel, -1.0)
        pose_ref[:, pl.ds(start, LANES)] = pos.astype(I32)
        full = jnp.concatenate([pos, jnp.full((LANES - N_EXPERTS, LANES), -1.0, F32)], axis=0)
        post_ref[pl.ds(start, LANES), :] = full.T
        off_ref[b] = jnp.broadcast_to(c_sel, (N_EXPERTS, LANES))
        return c_eq + inc_eq[:, LANES - 1:LANES], c_sel + inc_sel[:, LANES - 1:LANES]

    zero = jnp.zeros((N_EXPERTS, 1), F32)
    lax.fori_loop(0, nb, block, (zero, zero))


def _select(afft, cap):
    n = afft.shape[1]
    nb = n // LANES
    return pl.pallas_call(
        functools.partial(_select_kernel, cap=cap),
        out_shape=[jax.ShapeDtypeStruct((N_EXPERTS, n), I32), jax.ShapeDtypeStruct((n, LANES), F32),
                   jax.ShapeDtypeStruct((nb, N_EXPERTS, LANES), F32)],
        compiler_params=_params(None),
        name="select",
    )(afft)


def _gather_kernel(jt_ref, tt_ref, fl_ref, pos_ref, h_ref, o_ref, acc_scr, *, sb, kp):
    e = pl.program_id(0)
    k = pl.program_id(1)
    idx = e * kp + k
    j = jt_ref[idx]
    flags = fl_ref[idx]

    @pl.when((flags & 1) != 0)
    def _():
        tb = pos_ref.shape[2]
        kb = min(tb, 256)
        slot = lax.broadcasted_iota(I32, (sb, kb), 0) + j * sb
        rows = jnp.zeros((sb, D), F32)
        for t in range(tb // kb):
            onehot = jnp.where(pos_ref[0, :, t * kb:(t + 1) * kb] == slot, 1.0, 0.0).astype(BF16)
            rows = rows + _dot(onehot, h_ref[t * kb:(t + 1) * kb, :])

        @pl.when((flags & 2) != 0)
        def _():
            acc_scr[...] = rows

        @pl.when((flags & 2) == 0)
        def _():
            acc_scr[...] = acc_scr[...] + rows

    @pl.when((flags & 4) != 0)
    def _():
        o_ref[0] = acc_scr[...].astype(o_ref.dtype)


def _gather(jt, tt, fl, pose, h2, cap, sb, tb, kp):
    n = h2.shape[0]
    nt = n // tb
    pos3 = pose.reshape(N_EXPERTS * nt, 1, tb)
    grid_spec = pltpu.PrefetchScalarGridSpec(
        num_scalar_prefetch=3,
        grid=(N_EXPERTS, kp),
        in_specs=[
            pl.BlockSpec((1, 1, tb), lambda e, k, jt, tt, fl: (e * nt + tt[e * kp + k], 0, 0)),
            pl.BlockSpec((tb, D), lambda e, k, jt, tt, fl: (tt[e * kp + k], 0)),
        ],
        out_specs=pl.BlockSpec((1, sb, D), lambda e, k, jt, tt, fl: (e, jt[e * kp + k], 0)),
        scratch_shapes=[pltpu.VMEM((sb, D), F32)],
    )
    return pl.pallas_call(
        functools.partial(_gather_kernel, sb=sb, kp=kp),
        grid_spec=grid_spec,
        out_shape=jax.ShapeDtypeStruct((N_EXPERTS, cap, D), BF16),
        compiler_params=_params(("arbitrary", "arbitrary")),
        name="gather",
    )(jt, tt, fl, pos3, h2)


def _ffn_kernel(x_ref, w1_ref, w3_ref, w2_ref, o_ref, acc_scr):
    f = pl.program_id(1)
    nf = pl.num_programs(1)
    @pl.when(f == 0)
    def _():
        acc_scr[...] = jnp.zeros(acc_scr.shape, F32)

    w1 = w1_ref[0, 0].astype(BF16)
    w3 = w3_ref[0, 0].astype(BF16)
    w2 = w2_ref[0, 0].astype(BF16)
    cap = acc_scr.shape[0]
    rb = min(cap, 512)
    for r in range(cap // rb):
        x = x_ref[0, r * rb:(r + 1) * rb, :]
        he = (_silu(_dot(x, w1)) * _dot(x, w3)).astype(BF16)
        acc_scr[r * rb:(r + 1) * rb, :] = acc_scr[r * rb:(r + 1) * rb, :] + _dot(he, w2)

    @pl.when(f == nf - 1)
    def _():
        o_ref[0] = acc_scr[...].astype(o_ref.dtype)


def _ffn(xe, w1, w3, w2, l):
    cap = xe.shape[1]
    fc = 256
    return pl.pallas_call(
        _ffn_kernel,
        grid=(N_EXPERTS, EXPERT_FF // fc),
        in_specs=[
            pl.BlockSpec((1, cap, D), lambda e, f: (e, 0, 0)),
            pl.BlockSpec((1, 1, D, fc), lambda e, f: (l, e, 0, f)),
            pl.BlockSpec((1, 1, D, fc), lambda e, f: (l, e, 0, f)),
            pl.BlockSpec((1, 1, fc, D), lambda e, f: (l, e, f, 0)),
        ],
        out_specs=pl.BlockSpec((1, cap, D), lambda e, f: (e, 0, 0)),
        out_shape=jax.ShapeDtypeStruct((N_EXPERTS, cap, D), BF16),
        scratch_shapes=[pltpu.VMEM((cap, D), F32)],
        compiler_params=_params(("arbitrary", "arbitrary")),
        name="ffn",
    )(xe, w1, w3, w2)


def _combine_kernel(w0_ref, cnt_ref, pos_ref, gate_ref, x_ref, m5_ref, fw_ref, ye_hbm, o_ref, ybuf, xbuf, acc_scr, sem,
                    *, cap, win, final):
    b = pl.program_id(0)
    nblk = pl.num_programs(0)
    tb = pos_ref.shape[0]
    slot = lax.broadcasted_iota(I32, (tb, win), 1).astype(F32)
    half = b % 2

    def window(blk, e, wb):
        lo = (w0_ref[blk * N_EXPERTS + e] // BF16_ROWS) * BF16_ROWS + wb * win
        start = pl.multiple_of(jnp.minimum(lo, cap - win), BF16_ROWS)
        return lo, start

    def fetch(e, start, buf, s):
        return pltpu.make_async_copy(ye_hbm.at[e, pl.ds(start, win), :], buf, s)

    def first_window(blk, h, e):
        return fetch(e, window(blk, e, 0)[1], ybuf.at[h, e], sem.at[h * N_EXPERTS + e])

    @pl.when(b == 0)
    def _():
        for e in range(N_EXPERTS):
            first_window(0, 0, e).start()

    @pl.when(b + 1 < nblk)
    def _():
        for e in range(N_EXPERTS):
            first_window(b + 1, 1 - half, e).start()

    acc = jnp.zeros((tb, D), F32)
    for e in range(N_EXPERTS):
        first_window(b, half, e).wait()
        start = window(b, e, 0)[1]
        onehot = jnp.where(pos_ref[:, e:e + 1] - start.astype(F32) == slot, 1.0, 0.0).astype(BF16)
        acc = acc + _dot(onehot, ybuf[half, e]) * gate_ref[:, e:e + 1]
    acc_scr[...] = acc

    for e in range(N_EXPERTS):
        w0 = w0_ref[b * N_EXPERTS + e]
        n_win = (w0 % BF16_ROWS + cnt_ref[b * N_EXPERTS + e] + win - 1) // win

        def extra(wb, carry, e=e):
            lo, start = window(b, e, wb)
            cp = fetch(e, start, xbuf, sem.at[2 * N_EXPERTS])
            cp.start()
            cp.wait()
            posc = pos_ref[:, e:e + 1]
            hit = jnp.where(posc - start.astype(F32) == slot, 1.0, 0.0) * jnp.where(posc >= lo.astype(F32), 1.0, 0.0)
            acc_scr[...] = acc_scr[...] + _dot(hit.astype(BF16), xbuf[...]) * gate_ref[:, e:e + 1]
            return carry

        lax.fori_loop(1, n_win, extra, 0)

    y = x_ref[...] + m5_ref[...] * acc_scr[...]
    if final:
        y = y * lax.rsqrt(jnp.mean(y * y, axis=-1, keepdims=True) + EPS) * fw_ref[...]
    o_ref[...] = y


def _combine(w0, cnt, post, aff, ye, x, m5, fw, tb, final):
    n = x.shape[0]
    cap = ye.shape[1]
    win = min(COMBINE_WIN, cap)
    vec = pl.BlockSpec((1, D), lambda b, w0, cnt: (0, 0))
    grid_spec = pltpu.PrefetchScalarGridSpec(
        num_scalar_prefetch=2,
        grid=(n // tb,),
        in_specs=[
            pl.BlockSpec((tb, LANES), lambda b, w0, cnt: (b, 0)),
            pl.BlockSpec((tb, LANES), lambda b, w0, cnt: (b, 0)),
            pl.BlockSpec((tb, D), lambda b, w0, cnt: (b, 0)),
            vec, vec,
            pl.BlockSpec(memory_space=pl.ANY),
        ],
        out_specs=pl.BlockSpec((tb, D), lambda b, w0, cnt: (b, 0)),
        scratch_shapes=[pltpu.VMEM((2, N_EXPERTS, win, D), BF16), pltpu.VMEM((win, D), BF16),
                        pltpu.VMEM((tb, D), F32), pltpu.SemaphoreType.DMA((2 * N_EXPERTS + 1,))],
    )
    return pl.pallas_call(
        functools.partial(_combine_kernel, cap=cap, win=win, final=final),
        grid_spec=grid_spec,
        out_shape=jax.ShapeDtypeStruct((n, D), F32),
        compiler_params=_params(("arbitrary",)),
        name="combine",
    )(w0, cnt, post, aff, x, m5, fw, ye)


def _pair_tables(off_tb, cap, sb, kp):
    ne, nt = off_tb.shape
    nxt = jnp.concatenate([off_tb[:, 1:], jnp.full((ne, 1), cap, I32)], axis=1)
    cnt = nxt - off_tb
    j_lo = off_tb // sb
    j_hi = jnp.where(cnt > 0, (nxt - 1) // sb, j_lo - 1)
    npair = j_hi - j_lo + 1
    starts = jnp.cumsum(npair, axis=1) - npair
    total = jnp.sum(npair, axis=1, keepdims=True)
    k = jnp.arange(kp, dtype=I32)[None, :]
    kk = jnp.minimum(k, total - 1)
    tb_k = jnp.sum((starts[:, None, :] <= kk[:, :, None]).astype(I32), axis=2) - 1
    j_k = jnp.take_along_axis(j_lo, tb_k, axis=1) + kk - jnp.take_along_axis(starts, tb_k, axis=1)
    valid = (k < total).astype(I32)
    prev_j = jnp.concatenate([jnp.full((ne, 1), -1, I32), j_k[:, :-1]], axis=1)
    first = ((j_k != prev_j) & (k < total)).astype(I32)
    next_j = jnp.concatenate([j_k[:, 1:], jnp.full((ne, 1), -1, I32)], axis=1)
    last = (((j_k != next_j) | (k + 1 >= total)) & (k < total)).astype(I32)
    flags = valid + 2 * first + 4 * last
    return j_k.reshape(-1).astype(I32), tb_k.reshape(-1).astype(I32), flags.reshape(-1).astype(I32)


def _moe(x, nw, sh, sc, m5, rw, w1, w3, w2, l, fw, final):
    n = x.shape[0]
    cap = CAPACITY_FACTOR * n // N_EXPERTS
    sb = min(256, cap)
    tbg = min(1024, n)
    tbc = min(256, n)
    kp = cap // sb + n // tbg
    h2, aff, afft = _router(x, nw, sh, sc, rw)
    pose, post, off = _select(afft, cap)
    off128 = off[:, :, 0].T.astype(I32)
    jt, tt, fl = _pair_tables(off128[:, :: tbg // LANES], cap, sb, kp)
    xe = _gather(jt, tt, fl, pose, h2, cap, sb, tbg, kp)
    ye = _ffn(xe, w1, w3, w2, l)
    off_c = off128[:, :: tbc // LANES]
    cnt_c = jnp.concatenate([off_c[:, 1:], jnp.full((N_EXPERTS, 1), cap, I32)], axis=1) - off_c
    return _combine(off_c.T.reshape(-1), cnt_c.T.reshape(-1), post, aff, ye, x, m5, fw, tbc, final)


def _rope_tables(n_rows):
    row = jnp.repeat(jnp.arange(n_rows, dtype=F32), GRID_W)
    col = jnp.tile(jnp.arange(GRID_W, dtype=F32), n_rows)
    half = DIFF_HEAD_DIM // 2
    freqs = ROPE_BASE ** (-jnp.arange(0, half, 2, dtype=F32) / half)
    ar, ac = row[:, None] * freqs, col[:, None] * freqs
    cr, sr, cc, sn = jnp.cos(ar), jnp.sin(ar), jnp.cos(ac), jnp.sin(ac)
    z = jnp.zeros_like(sr)
    cos = jnp.concatenate([cr, cr, cc, cc], axis=1)
    sa = jnp.concatenate([-sr, z, -sn, z], axis=1)
    sb = jnp.concatenate([z, sr, z, sn], axis=1)
    return tuple(jnp.tile(t, (1, 2)) for t in (cos, sa, sb))


def _identity_tables(n):
    return (jnp.ones((n, LANES), F32), jnp.zeros((n, LANES), F32), jnp.zeros((n, LANES), F32))


def _proj_weight(w_in):
    sizes = (1024, 256, 32, 1024, 1024, 256, 1024, 1024, 2048, 3072)
    idx = np.cumsum((0,) + sizes)
    px, pb, pdt, pk, pv, pc, pq, pz, pglu, pgate = [w_in[:, idx[i]:idx[i + 1]] for i in range(10)]
    pad = jnp.zeros((D, LANES - SSD_HEADS), w_in.dtype)
    tail = jnp.zeros((D, N_PROJ_F32 - OFF_DT - 2 * LANES), w_in.dtype)
    w = jnp.concatenate([px, pz, pglu, pgate, pb, pc, pdt[:, :SSD_HEADS], pad, pdt[:, SSD_HEADS:], pad, tail,
                         pk, pv, pq], axis=1)
    return w.astype(BF16)


def _lane_pad(v):
    return jnp.pad(v, ((0, 0), (0, LANES - v.shape[1]))).reshape(2, 1, LANES)


def kernel(x, c, ctx, c_ctx, ada_w, ada_b, norm1_w, norm2_w, w_in, ssd_conv_w, ssd_conv_b, ssd_dt_bias, ssd_a_log, ssd_d, ssd_norm_w, ssd_out, diff_lambda, diff_subln_w, diff_out, conf_dw_w, conf_dw_b, conf_ln_w, conf_ln_b, conf_out, w_o, router_w, exp_w1, exp_w3, exp_w2, final_norm_w):
    depth = ada_w.shape[0]
    n = x.shape[1]
    nctx = ctx.shape[1]
    x_lat = x[0]
    x_ctx = ctx[0]

    cc = jnp.zeros((8, D), F32).at[0].set(c[0]).at[1].set(c_ctx)
    mods = _ada(cc, ada_w, ada_b)

    rope = _rope_tables(n // GRID_W)
    ident_ctx = _identity_tables(nctx)
    scale = DIFF_HEAD_DIM ** -0.5 * math.log2(math.e)
    zero_state = jnp.zeros((2, SSD_STATE, SSD_HEADS * SSD_HEAD_DIM), F32)
    fw = final_norm_w.reshape(1, D)

    for l in range(depth):
        last = l == depth - 1
        lam_init = 0.8 - 0.6 * math.exp(-0.3 * l)
        lq1, lk1, lq2, lk2 = diff_lambda[l].astype(F32)
        lam = (jnp.exp(jnp.sum(lq1 * lk1)) - jnp.exp(jnp.sum(lq2 * lk2)) + lam_init).reshape(1)
        m_lat = [mods[l, 0:1, i * D:(i + 1) * D] for i in range(N_MOD)]
        m_ctx = [mods[l, 1:2, i * D:(i + 1) * D] for i in range(N_MOD)]
        nw1 = norm1_w[l].reshape(1, D)
        nw2 = norm2_w[l].reshape(1, D)
        w = _proj_weight(w_in[l])
        dtb = _lane_pad(ssd_dt_bias[l])
        alog = _lane_pad(ssd_a_log[l])
        dsk = jnp.repeat(ssd_d[l], SSD_HEAD_DIM).reshape(1, D)
        gnw = ssd_norm_w[l].reshape(1, D)
        sw = diff_subln_w[l].reshape(1, LANES)
        wa, wb, wc, wo = (t[l].astype(BF16) for t in (ssd_out, diff_out, conf_out, w_o))
        rw = jnp.pad(router_w[l], ((0, 0), (0, LANES - N_EXPERTS))).astype(BF16)
        coef = 1.0 - lam_init

        k_all = jnp.zeros((n + nctx, D), BF16)
        v_all = jnp.zeros((n + nctx, D), BF16)
        p_l, k_all, v_all, q_l = _inproj(x_lat, nw1, m_lat[0], m_lat[1], w, rope, scale, k_all, v_all, 0)
        p_c, k_all, v_all, q_c = _inproj(x_ctx, nw1, m_ctx[0], m_ctx[1], w, ident_ctx, scale, k_all, v_all, n // nctx)
        xbc_c = _ssdconv(p_c, ssd_conv_w[l], ssd_conv_b[l])
        xbc_l = _ssdconv(p_l, ssd_conv_w[l], ssd_conv_b[l])
        yf_c, yb_c, st_c = _ssd(xbc_c, p_c, dtb, alog, zero_state)

        if not last:
            on_c = _attention(lam, q_c, k_all, v_all, sw, nctx, n // nctx, coef)
            cn_c = _conformer(p_c, conf_dw_w[l], conf_dw_b[l], conf_ln_w[l], conf_ln_b[l])
            x_ctx = _merge(yf_c, yb_c, xbc_c, p_c, on_c, cn_c, x_ctx, dsk, gnw, m_ctx[2], wa, wb, wc, wo)
            x_ctx = _moe(x_ctx, nw2, m_ctx[3], m_ctx[4], m_ctx[5], rw, exp_w1, exp_w3, exp_w2, l, fw, False)

        yf_l, yb_l, _ = _ssd(xbc_l, p_l, dtb, alog, st_c)
        on_l = _attention(lam, q_l, k_all, v_all, sw, nctx + n, 0, coef)
        cn_l = _conformer(p_l, conf_dw_w[l], conf_dw_b[l], conf_ln_w[l], conf_ln_b[l])
        x_lat = _merge(yf_l, yb_l, xbc_l, p_l, on_l, cn_l, x_lat, dsk, gnw, m_lat[2], wa, wb, wc, wo)
        x_lat = _moe(x_lat, nw2, m_lat[3], m_lat[4], m_lat[5], rw, exp_w1, exp_w3, exp_w2, l, fw, last)

    return x_lat[None]
```

```python
import functools
import math

import jax
import jax.numpy as jnp
import numpy as np
from jax import lax
from jax.experimental import pallas as pl
from jax.experimental.pallas import tpu as pltpu

F32 = jnp.float32
BF16 = jnp.bfloat16
I32 = jnp.int32

EPS = 1e-6
D = 1024
N_MOD = 6
GRID_W = 64
SSD_HEADS = 16
SSD_HEAD_DIM = 64
SSD_STATE = 128
SSD_GN = 256
SSD_CHUNK = 128
DIFF_HEADS = 8
DIFF_HEAD_DIM = 64
ROPE_BASE = 10000.0
CONF_KERNEL = 31
CONF_HALO = 16
N_EXPERTS = 16
EXPERT_FF = 2048
CAPACITY_FACTOR = 2
LANES = 128
BF16_ROWS = 16
COMBINE_WIN = 128

OFF_X, OFF_Z, OFF_GLU_A, OFF_GLU_G, OFF_GATE = 0, 1024, 2048, 3072, 4096
OFF_B, OFF_C, OFF_DT = 7168, 7424, 7680
N_PROJ_F32 = 8192
OFF_K, OFF_V, OFF_Q = 8192, 9216, 10240
N_PROJ = 11264

VMEM_LIMIT = 56 * 1024 * 1024


def _params(sem, vmem=VMEM_LIMIT):
    return pltpu.CompilerParams(dimension_semantics=sem, vmem_limit_bytes=vmem)


def _silu(x):
    return x * jax.nn.sigmoid(x)


def _split3(x):
    a1 = x.astype(BF16)
    r1 = x - a1.astype(F32)
    a2 = r1.astype(BF16)
    a3 = (r1 - a2.astype(F32)).astype(BF16)
    return a1, a2, a3


def _dot(a, b):
    return jnp.dot(a, b, preferred_element_type=F32)


def _ada_kernel(c_ref, w_ref, b_ref, o_ref):
    a = _silu(c_ref[...]).astype(BF16)
    o_ref[0] = _dot(a, w_ref[0].astype(BF16)) + b_ref[0]


def _ada(cc, ada_w, ada_b):
    nl = ada_w.shape[0]
    tn = 1536
    return pl.pallas_call(
        _ada_kernel,
        grid=(nl, N_MOD * D // tn),
        in_specs=[
            pl.BlockSpec((8, D), lambda l, j: (0, 0)),
            pl.BlockSpec((1, D, tn), lambda l, j: (l, 0, j)),
            pl.BlockSpec((1, 1, tn), lambda l, j: (l, 0, j)),
        ],
        out_specs=pl.BlockSpec((1, 8, tn), lambda l, j: (l, 0, j)),
        out_shape=jax.ShapeDtypeStruct((nl, 8, N_MOD * D), F32),
        compiler_params=_params(("arbitrary", "arbitrary")),
        name="ada",
    )(cc, ada_w, ada_b.reshape(nl, 1, N_MOD * D))


def _modulate(x, nw, sh, sc):
    r = lax.rsqrt(jnp.mean(x * x, axis=-1, keepdims=True) + EPS)
    return (x * r * nw) * (1.0 + sc) + sh


def _rope(y, cos, sa, sb):
    parts = []
    for j in range(D // LANES):
        yj = y[:, j * LANES:(j + 1) * LANES]
        parts.append(yj * cos + pltpu.roll(yj, LANES - 16, 1) * sa + pltpu.roll(yj, 16, 1) * sb)
    return jnp.concatenate(parts, axis=1)


def _inproj_kernel(x_ref, nw_ref, sh_ref, sc_ref, w_ref, cos_ref, sa_ref, sb_ref, k_in, v_in,
                   p_ref, k_ref, v_ref, q_ref, h_scr, *, scale):
    del k_in, v_in
    j = pl.program_id(1)
    nf = N_PROJ_F32 // D

    @pl.when(j == 0)
    def _():
        h_scr[...] = _modulate(x_ref[...], nw_ref[...], sh_ref[...], sc_ref[...]).astype(BF16)

    y = _dot(h_scr[...], w_ref[...])

    @pl.when(j < nf)
    def _():
        p_ref[...] = y

    @pl.when(j == nf)
    def _():
        k_ref[...] = _rope(y, cos_ref[...], sa_ref[...], sb_ref[...]).astype(BF16)

    @pl.when(j == nf + 1)
    def _():
        v_ref[...] = y.astype(BF16)

    @pl.when(j == nf + 2)
    def _():
        q_ref[...] = (_rope(y, cos_ref[...], sa_ref[...], sb_ref[...]) * scale).astype(BF16)


def _inproj(x, nw, sh, sc, w, tabs, scale, k_all, v_all, row_block):
    n = x.shape[0]
    tm = min(n, 1024)
    tn = D
    nf = N_PROJ_F32 // tn
    vec = pl.BlockSpec((1, D), lambda i, j: (0, 0))
    tab = pl.BlockSpec((tm, LANES), lambda i, j: (i, 0))
    kv_spec = pl.BlockSpec((tm, D), lambda i, j: (row_block + i, 0))
    kv_shape = jax.ShapeDtypeStruct(k_all.shape, BF16)
    hbm = pl.BlockSpec(memory_space=pl.ANY)
    return pl.pallas_call(
        functools.partial(_inproj_kernel, scale=scale),
        grid=(n // tm, N_PROJ // tn),
        in_specs=[pl.BlockSpec((tm, D), lambda i, j: (i, 0)), vec, vec, vec,
                  pl.BlockSpec((D, tn), lambda i, j: (0, j)), tab, tab, tab, hbm, hbm],
        out_specs=[pl.BlockSpec((tm, tn), lambda i, j: (i, jnp.minimum(j, nf - 1))), kv_spec, kv_spec,
                   pl.BlockSpec((tm, D), lambda i, j: (i, 0))],
        out_shape=[jax.ShapeDtypeStruct((n, N_PROJ_F32), F32), kv_shape, kv_shape,
                   jax.ShapeDtypeStruct((n, D), BF16)],
        scratch_shapes=[pltpu.VMEM((tm, D), BF16)],
        input_output_aliases={8: 1, 9: 2},
        compiler_params=_params(("arbitrary", "arbitrary")),
        name="inproj",
    )(x, nw, sh, sc, w, *tabs, k_all, v_all)


def _ssdconv_kernel(prev_ref, cur_ref, next_ref, w_ref, b_ref, o_ref):
    i = pl.program_id(0)
    nt = pl.num_programs(0)
    u = cur_ref[...]
    t = u.shape[0]
    row = lax.broadcasted_iota(I32, u.shape, 0)
    before = jnp.where(i > 0, prev_ref[7:8, :], 0.0)
    after = jnp.where(i < nt - 1, next_ref[0:1, :], 0.0)
    um1 = jnp.where(row == 0, before, pltpu.roll(u, 1, 0))
    up1 = jnp.where(row == t - 1, after, pltpu.roll(u, t - 1, 0))
    y = um1 * w_ref[0:1, :] + u * w_ref[1:2, :] + up1 * w_ref[2:3, :] + b_ref[...]
    o_ref[...] = _silu(y)


def _ssdconv(p, conv_w, conv_b):
    n = p.shape[0]
    t = min(n, 512)
    cw = 512
    r8 = t // 8
    nb8 = n // 8

    def col(c):
        return jnp.where(c < 2, c, OFF_B // cw)

    return pl.pallas_call(
        _ssdconv_kernel,
        grid=(n // t, 3),
        in_specs=[
            pl.BlockSpec((8, cw), lambda i, c: (jnp.maximum(i * r8 - 1, 0), col(c))),
            pl.BlockSpec((t, cw), lambda i, c: (i, col(c))),
            pl.BlockSpec((8, cw), lambda i, c: (jnp.minimum((i + 1) * r8, nb8 - 1), col(c))),
            pl.BlockSpec((3, cw), lambda i, c: (0, c)),
            pl.BlockSpec((1, cw), lambda i, c: (0, c)),
        ],
        out_specs=pl.BlockSpec((t, cw), lambda i, c: (i, c)),
        out_shape=jax.ShapeDtypeStruct((n, 3 * cw), F32),
        compiler_params=_params(("arbitrary", "arbitrary")),
        name="ssdconv",
    )(p, p, p, conv_w, conv_b.reshape(1, -1))


def _ssd_direction(d, xs_ref, b_ref, c_ref, dt_ref, dtb, alog, st_ref, y_ref):
    L = SSD_CHUNK
    P = SSD_HEAD_DIM
    HP = SSD_HEADS * P
    GW = HP // 2

    x = dt_ref[...] + dtb
    dt = jnp.maximum(x, 0.0) + jnp.log1p(jnp.exp(-jnp.abs(x)))
    a = dt * (-jnp.exp(alog))

    li = lax.broadcasted_iota(I32, (L, L), 0)
    si = lax.broadcasted_iota(I32, (L, L), 1)
    keep = si <= li if d == 0 else si >= li
    tri = jnp.where(keep, 1.0, 0.0).astype(BF16)
    a1, a2, a3 = _split3(a)
    cum = _dot(tri, a1) + _dot(tri, a2) + _dot(tri, a3)
    cum_t = cum.T

    ex = jnp.where(lax.broadcasted_iota(I32, (LANES, HP), 0) == lax.broadcasted_iota(I32, (LANES, HP), 1) // P,
                   1.0, 0.0).astype(BF16)

    def expand(v):
        v1, v2, v3 = _split3(v)
        return _dot(v1, ex) + _dot(v2, ex) + _dot(v3, ex)

    cum_e = expand(cum)
    dt_e = expand(dt)
    tot_e = cum_e[L - 1:L, :] if d == 0 else cum_e[0:1, :]

    xdt = xs_ref[...] * dt_e
    xb = xdt.astype(BF16)
    xw = (xdt * jnp.exp(tot_e - cum_e)).astype(BF16)
    bb = b_ref[...].astype(BF16)
    cb = c_ref[...].astype(BF16)
    st = st_ref[...]
    stb = st.astype(BF16)
    grow = jnp.exp(cum_e)

    for g in range(2):
        bg = bb[:, g * SSD_STATE:(g + 1) * SSD_STATE]
        cg = cb[:, g * SSD_STATE:(g + 1) * SSD_STATE]
        scores = lax.dot_general(cg, bg, (((1,), (1,)), ((), ())), preferred_element_type=F32)
        y_off = _dot(cg, stb[:, g * GW:(g + 1) * GW]) * grow[:, g * GW:(g + 1) * GW]
        for hh in range(SSD_HEADS // 2):
            h = g * (SSD_HEADS // 2) + hh
            seg = cum[:, h:h + 1] - cum_t[h:h + 1, :]
            dec = jnp.where(keep, jnp.exp(jnp.where(keep, seg, 0.0)), 0.0)
            m = (scores * dec).astype(BF16)
            yd = _dot(m, xb[:, h * P:(h + 1) * P])
            y_ref[:, h * P:(h + 1) * P] = yd + y_off[:, hh * P:(hh + 1) * P]
        upd = lax.dot_general(bg, xw[:, g * GW:(g + 1) * GW], (((0,), (0,)), ((), ())),
                              preferred_element_type=F32)
        st_ref[:, g * GW:(g + 1) * GW] = st[:, g * GW:(g + 1) * GW] * jnp.exp(tot_e[:, g * GW:(g + 1) * GW]) + upd


def _ssd_kernel(xsf_ref, bf_ref, cf_ref, dtf_ref, xsb_ref, bb_ref, cb_ref, dtb_ref, bias_ref, alog_ref, h0_ref,
                yf_ref, yb_ref, hT_ref, st_scr):
    c = pl.program_id(0)

    @pl.when(c == 0)
    def _():
        st_scr[...] = h0_ref[...]

    L = SSD_CHUNK
    nsub = xsf_ref.shape[0] // L
    for k in range(nsub):
        f = slice(k * L, (k + 1) * L)
        r = slice((nsub - 1 - k) * L, (nsub - k) * L)
        _ssd_direction(0, xsf_ref.at[f], bf_ref.at[f], cf_ref.at[f], dtf_ref.at[f], bias_ref[0], alog_ref[0],
                       st_scr.at[0], yf_ref.at[f])
        _ssd_direction(1, xsb_ref.at[r], bb_ref.at[r], cb_ref.at[r], dtb_ref.at[r], bias_ref[1], alog_ref[1],
                       st_scr.at[1], yb_ref.at[r])

    @pl.when(c == pl.num_programs(0) - 1)
    def _():
        hT_ref[...] = st_scr[...]


def _ssd(xbc, p, dtb, alog, h0):
    n = xbc.shape[0]
    L = SSD_CHUNK * (2 if n % (2 * SSD_CHUNK) == 0 else 1)
    nc = n // L
    HP = SSD_HEADS * SSD_HEAD_DIM
    full = pl.BlockSpec((2, SSD_STATE, HP), lambda c: (0, 0, 0))
    vec = pl.BlockSpec((2, 1, LANES), lambda c: (0, 0, 0))

    def side(chunk, d):
        return [
            pl.BlockSpec((L, HP), lambda c: (chunk(c), 0)),
            pl.BlockSpec((L, SSD_GN), lambda c: (chunk(c), HP // SSD_GN)),
            pl.BlockSpec((L, SSD_GN), lambda c: (chunk(c), HP // SSD_GN + 1)),
            pl.BlockSpec((L, LANES), lambda c: (chunk(c), OFF_DT // LANES + d)),
        ]

    def fwd(c):
        return c

    def bwd(c):
        return nc - 1 - c

    return pl.pallas_call(
        _ssd_kernel,
        grid=(nc,),
        in_specs=side(fwd, 0) + side(bwd, 1) + [vec, vec, full],
        out_specs=[pl.BlockSpec((L, HP), lambda c: (c, 0)), pl.BlockSpec((L, HP), lambda c: (nc - 1 - c, 0)), full],
        out_shape=[jax.ShapeDtypeStruct((n, HP), F32), jax.ShapeDtypeStruct((n, HP), F32),
                   jax.ShapeDtypeStruct((2, SSD_STATE, HP), F32)],
        scratch_shapes=[pltpu.VMEM((2, SSD_STATE, HP), F32)],
        compiler_params=_params(("arbitrary",)),
        name="ssd",
    )(xbc, xbc, xbc, p, xbc, xbc, xbc, p, dtb, alog, h0)


def _attn_kernel(lam_ref, q_ref, k_ref, v_ref, sw_ref, o_ref, q2_scr, sa_scr, sb_scr, p_scr, m_scr, acc_scr,
                 *, tq, nq, ck, nchunks, coef):
    lane = lax.broadcasted_iota(I32, (tq, LANES), 1)
    for t in range(nq):
        q = q_ref[t * tq:(t + 1) * tq, :]
        zero = jnp.zeros_like(q)
        q2_scr[t, 0:tq, :] = jnp.where(lane < DIFF_HEAD_DIM, q, zero)
        q2_scr[t, tq:2 * tq, :] = jnp.where(lane >= DIFF_HEAD_DIM, q, zero)
    m_scr[...] = jnp.full(m_scr.shape, -jnp.inf, F32)
    acc_scr[...] = jnp.zeros(acc_scr.shape, F32)
    ones = jnp.ones((ck, LANES), BF16)
    nb = ck // LANES

    def key_rows(c):
        return pl.ds(c * ck, ck) if isinstance(c, int) else pl.ds(pl.multiple_of(c * ck, ck), ck)

    def scores(t, c, s_scr):
        s_scr[...] = lax.dot_general(q2_scr[t], k_ref[key_rows(c), :], (((1,), (1,)), ((), ())),
                                     preferred_element_type=F32)

    def soft_pv(t, c, s_scr):
        rows = key_rows(c)
        mx = s_scr[:, 0:LANES]
        for b in range(1, nb):
            mx = jnp.maximum(mx, s_scr[:, b * LANES:(b + 1) * LANES])
        m_prev = m_scr[t]
        m_new = jnp.maximum(m_prev, jnp.max(mx, axis=1, keepdims=True))
        alpha = jnp.exp2(m_prev - m_new)
        for b in range(nb):
            p_scr[:, b * LANES:(b + 1) * LANES] = jnp.exp2(s_scr[:, b * LANES:(b + 1) * LANES] - m_new).astype(BF16)
        va = jnp.concatenate([v_ref[rows, :], ones], axis=1)
        acc_scr[t] = acc_scr[t] * jnp.concatenate([alpha, alpha], axis=1) + _dot(p_scr[...], va)
        m_scr[t] = m_new

    def tile(t, cur, nxt):
        def pair(jj, carry):
            scores(t, 2 * jj + 1, nxt)
            soft_pv(t, 2 * jj, cur)
            scores(t, 2 * jj + 2, cur)
            soft_pv(t, 2 * jj + 1, nxt)
            return carry

        lax.fori_loop(0, (nchunks - 1) // 2, pair, 0)
        last, free = cur, nxt
        if (nchunks - 1) % 2 == 1:
            scores(t, nchunks - 1, nxt)
            soft_pv(t, nchunks - 2, cur)
            last, free = nxt, cur
        if t + 1 < nq:
            scores(t + 1, 0, free)
        soft_pv(t, nchunks - 1, last)
        return free, last

    scores(0, 0, sa_scr)
    bufs = (sa_scr, sb_scr)
    for t in range(nq):
        bufs = tile(t, *bufs)

    for t in range(nq):
        on = acc_scr[t, :, 0:LANES] / acc_scr[t, :, LANES:2 * LANES]
        o = on[0:tq, :] - lam_ref[0] * on[tq:2 * tq, :]
        r = lax.rsqrt(jnp.mean(o * o, axis=-1, keepdims=True) + EPS)
        o_ref[t * tq:(t + 1) * tq, :] = ((o * r * sw_ref[...]) * coef).astype(o_ref.dtype)


def _pick_tk(m):
    for cand in (1280, 1024, 512, 256):
        if m % cand == 0:
            return cand
    return m


def _attention(lam, q, k, v, subln_w, m, key_block, coef):
    n = q.shape[0]
    tq = min(n, 512)
    nq = min(2, n // tq)
    ck = _pick_tk(m)
    return pl.pallas_call(
        functools.partial(_attn_kernel, tq=tq, nq=nq, ck=ck, nchunks=m // ck, coef=coef),
        grid=(DIFF_HEADS, n // (nq * tq)),
        in_specs=[
            pl.BlockSpec(memory_space=pltpu.SMEM),
            pl.BlockSpec((nq * tq, LANES), lambda h, i: (i, h)),
            pl.BlockSpec((m, LANES), lambda h, i: (key_block, h)),
            pl.BlockSpec((m, LANES), lambda h, i: (key_block, h)),
            pl.BlockSpec((1, LANES), lambda h, i: (0, 0)),
        ],
        out_specs=pl.BlockSpec((nq * tq, LANES), lambda h, i: (i, h)),
        out_shape=jax.ShapeDtypeStruct((n, D), BF16),
        scratch_shapes=[pltpu.VMEM((nq, 2 * tq, LANES), BF16), pltpu.VMEM((2 * tq, ck), F32),
                        pltpu.VMEM((2 * tq, ck), F32), pltpu.VMEM((2 * tq, ck), BF16),
                        pltpu.VMEM((nq, 2 * tq, LANES), F32), pltpu.VMEM((nq, 2 * tq, 2 * LANES), F32)],
        compiler_params=_params(("arbitrary", "arbitrary")),
        name="attn",
    )(lam, q, k, v, subln_w)


def _conf_kernel(ap_ref, gp_ref, a_ref, g_ref, an_ref, gn_ref, w_ref, b_ref, lw_ref, lb_ref, o_ref, ext_scr, sh_scr):
    i = pl.program_id(0)
    nt = pl.num_programs(0)
    t = a_ref.shape[0]
    hl = CONF_HALO

    def glu(a, g):
        return a[...] * jax.nn.sigmoid(g[...])

    ext_scr[0:hl, :] = jnp.where(i > 0, glu(ap_ref, gp_ref), 0.0)
    ext_scr[hl:hl + t, :] = glu(a_ref, g_ref)
    ext_scr[hl + t:2 * hl + t, :] = jnp.where(i < nt - 1, glu(an_ref, gn_ref), 0.0)

    span = t + 2 * hl - 8
    for r in range(1, 8):
        sh_scr[r, 0:span, :] = ext_scr[pl.ds(r, span), :]
    acc = jnp.zeros((t, D), F32) + b_ref[...]
    for k in range(CONF_KERNEL):
        off = k + hl - CONF_KERNEL // 2
        src = ext_scr if off % 8 == 0 else sh_scr.at[off % 8]
        acc = acc + src[pl.ds(off - off % 8, t), :] * w_ref[k:k + 1, :]
    mu = jnp.mean(acc, axis=-1, keepdims=True)
    xc = acc - mu
    y = xc * lax.rsqrt(jnp.mean(xc * xc, axis=-1, keepdims=True) + EPS) * lw_ref[...] + lb_ref[...]
    o_ref[...] = _silu(y).astype(o_ref.dtype)


def _conformer(p, dw_w, dw_b, ln_w, ln_b):
    n = p.shape[0]
    t = min(n, 256)
    hl = CONF_HALO
    rh = t // hl
    nbh = n // hl
    ca, cg = OFF_GLU_A // D, OFF_GLU_G // D

    def prev(i):
        return jnp.maximum(i * rh - 1, 0)

    def nxt(i):
        return jnp.minimum((i + 1) * rh, nbh - 1)

    vec = pl.BlockSpec((1, D), lambda i: (0, 0))
    return pl.pallas_call(
        _conf_kernel,
        grid=(n // t,),
        in_specs=[
            pl.BlockSpec((hl, D), lambda i: (prev(i), ca)), pl.BlockSpec((hl, D), lambda i: (prev(i), cg)),
            pl.BlockSpec((t, D), lambda i: (i, ca)), pl.BlockSpec((t, D), lambda i: (i, cg)),
            pl.BlockSpec((hl, D), lambda i: (nxt(i), ca)), pl.BlockSpec((hl, D), lambda i: (nxt(i), cg)),
            pl.BlockSpec((CONF_KERNEL, D), lambda i: (0, 0)), vec, vec, vec,
        ],
        out_specs=pl.BlockSpec((t, D), lambda i: (i, 0)),
        out_shape=jax.ShapeDtypeStruct((n, D), BF16),
        scratch_shapes=[pltpu.VMEM((t + 2 * hl, D), F32), pltpu.VMEM((8, t + 2 * hl, D), F32)],
        compiler_params=_params(("arbitrary",)),
        name="conformer",
    )(p, p, p, p, p, p, dw_w, dw_b.reshape(1, D), ln_w.reshape(1, D), ln_b.reshape(1, D))


def _merge_kernel(yf_ref, yb_ref, xs_ref, z_ref, on_ref, cn_ref, g0_ref, g1_ref, g2_ref, x_ref, dsk_ref, nw_ref,
                  m2_ref, wa_ref, wb_ref, wc_ref, wo_ref, o_ref):
    y = yf_ref[...] + yb_ref[...] + xs_ref[...] * dsk_ref[...]
    yz = y * _silu(z_ref[...])
    half = D // 2
    parts = []
    for g in range(2):
        seg = yz[:, g * half:(g + 1) * half]
        r = lax.rsqrt(jnp.mean(seg * seg, axis=-1, keepdims=True) + EPS)
        parts.append(seg * r * nw_ref[:, g * half:(g + 1) * half])
    gn = jnp.concatenate(parts, axis=1).astype(BF16)
    y_a = _dot(gn, wa_ref[...])
    y_b = _dot(on_ref[...], wb_ref[...])
    y_c = _dot(cn_ref[...], wc_ref[...])
    m = jax.nn.sigmoid(g0_ref[...]) * y_a + jax.nn.sigmoid(g1_ref[...]) * y_b + jax.nn.sigmoid(g2_ref[...]) * y_c
    y_o = _dot(m.astype(BF16), wo_ref[...])
    o_ref[...] = x_ref[...] + m2_ref[...] * y_o


def _merge(yf, yb, xbc, p, on, cn, x, dsk, nw, m2, wa, wb, wc, wo):
    n = x.shape[0]
    tm = min(n, 256)
    vec = pl.BlockSpec((1, D), lambda i: (0, 0))
    wsp = pl.BlockSpec((D, D), lambda i: (0, 0))
    row = pl.BlockSpec((tm, D), lambda i: (i, 0))
    return pl.pallas_call(
        _merge_kernel,
        grid=(n // tm,),
        in_specs=[
            row, row,
            row,
            pl.BlockSpec((tm, D), lambda i: (i, OFF_Z // D)),
            row, row,
            pl.BlockSpec((tm, D), lambda i: (i, OFF_GATE // D)),
            pl.BlockSpec((tm, D), lambda i: (i, OFF_GATE // D + 1)),
            pl.BlockSpec((tm, D), lambda i: (i, OFF_GATE // D + 2)),
            row, vec, vec, vec, wsp, wsp, wsp, wsp,
        ],
        out_specs=row,
        out_shape=jax.ShapeDtypeStruct((n, D), F32),
        compiler_params=_params(("arbitrary",)),
        name="merge",
    )(yf, yb, xbc, p, on, cn, p, p, p, x, dsk, nw, m2, wa, wb, wc, wo)


def _router_kernel(x_ref, nw_ref, sh_ref, sc_ref, rw_ref, h_ref, aff_ref, afft_ref):
    h = _modulate(x_ref[...], nw_ref[...], sh_ref[...], sc_ref[...]).astype(BF16)
    h_ref[...] = h
    logits = _dot(h, rw_ref[...])
    lane = lax.broadcasted_iota(I32, logits.shape, 1)
    logits = jnp.where(lane < N_EXPERTS, logits, -jnp.inf)
    e = jnp.exp(logits - jnp.max(logits, axis=-1, keepdims=True))
    aff = e / jnp.sum(e, axis=-1, keepdims=True)
    aff_ref[...] = aff
    afft_ref[...] = aff.T[0:N_EXPERTS, :]


def _router(x, nw, sh, sc, rw):
    n = x.shape[0]
    tm = min(n, 512)
    vec = pl.BlockSpec((1, D), lambda i: (0, 0))
    return pl.pallas_call(
        _router_kernel,
        grid=(n // tm,),
        in_specs=[pl.BlockSpec((tm, D), lambda i: (i, 0)), vec, vec, vec, pl.BlockSpec((D, LANES), lambda i: (0, 0))],
        out_specs=[pl.BlockSpec((tm, D), lambda i: (i, 0)), pl.BlockSpec((tm, LANES), lambda i: (i, 0)),
                   pl.BlockSpec((N_EXPERTS, tm), lambda i: (0, i))],
        out_shape=[jax.ShapeDtypeStruct((n, D), BF16), jax.ShapeDtypeStruct((n, LANES), F32),
                   jax.ShapeDtypeStruct((N_EXPERTS, n), F32)],
        compiler_params=_params(("arbitrary",)),
        name="router",
    )(x, nw, sh, sc, rw)


def _select_kernel(a_ref, pose_ref, post_ref, off_ref, *, cap):
    n = a_ref.shape[1]
    nb = n // LANES
    def search(it, thr):
        bits = pltpu.bitcast(a_ref[...], I32)
        cand = thr | lax.shift_left(jnp.int32(1), 30 - it)
        cnt = jnp.sum(jnp.where(bits >= cand, 1.0, 0.0), axis=1, keepdims=True)
        return jnp.where(cnt >= cap, cand, thr)

    thr = lax.fori_loop(0, 31, search, jnp.zeros((N_EXPERTS, 1), I32))
    n_gt = jnp.sum(jnp.where(pltpu.bitcast(a_ref[...], I32) > thr, 1.0, 0.0), axis=1, keepdims=True)
    need = cap - n_gt

    tri = jnp.where(lax.broadcasted_iota(I32, (LANES, LANES), 0) <= lax.broadcasted_iota(I32, (LANES, LANES), 1),
                    1.0, 0.0).astype(BF16)

    def block(b, carry):
        c_eq, c_sel = carry
        start = pl.multiple_of(b * LANES, LANES)
        bb = pltpu.bitcast(a_ref[:, pl.ds(start, LANES)], I32)
        gt = jnp.where(bb > thr, 1.0, 0.0)
        eq = jnp.where(bb == thr, 1.0, 0.0)
        inc_eq = _dot(eq.astype(BF16), tri)
        rank_eq = c_eq + inc_eq - eq
        sel = gt + eq * jnp.where(rank_eq < need, 1.0, 0.0)
        inc_sel = _dot(sel.astype(BF16), tri)
        pos = jnp.where(sel > 0.0, c_sel + inc_sel - sel, -1.0)
        pose_ref[:, pl.ds(start, LANES)] = pos.astype(I32)
        full = jnp.concatenate([pos, jnp.full((LANES - N_EXPERTS, LANES), -1.0, F32)], axis=0)
        post_ref[pl.ds(start, LANES), :] = full.T
        off_ref[b] = jnp.broadcast_to(c_sel, (N_EXPERTS, LANES))
        return c_eq + inc_eq[:, LANES - 1:LANES], c_sel + inc_sel[:, LANES - 1:LANES]

    zero = jnp.zeros((N_EXPERTS, 1), F32)
    lax.fori_loop(0, nb, block, (zero, zero))


def _select(afft, cap):
    n = afft.shape[1]
    nb = n // LANES
    return pl.pallas_call(
        functools.partial(_select_kernel, cap=cap),
        out_shape=[jax.ShapeDtypeStruct((N_EXPERTS, n), I32), jax.ShapeDtypeStruct((n, LANES), F32),
                   jax.ShapeDtypeStruct((nb, N_EXPERTS, LANES), F32)],
        compiler_params=_params(None),
        name="select",
    )(afft)


def _gather_kernel(jt_ref, tt_ref, fl_ref, pos_ref, h_ref, o_ref, acc_scr, *, sb, kp):
    e = pl.program_id(0)
    k = pl.program_id(1)
    idx = e * kp + k
    j = jt_ref[idx]
    flags = fl_ref[idx]

    @pl.when((flags & 1) != 0)
    def _():
        tb = pos_ref.shape[2]
        kb = min(tb, 256)
        slot = lax.broadcasted_iota(I32, (sb, kb), 0) + j * sb
        rows = jnp.zeros((sb, D), F32)
        for t in range(tb // kb):
            onehot = jnp.where(pos_ref[0, :, t * kb:(t + 1) * kb] == slot, 1.0, 0.0).astype(BF16)
            rows = rows + _dot(onehot, h_ref[t * kb:(t + 1) * kb, :])

        @pl.when((flags & 2) != 0)
        def _():
            acc_scr[...] = rows

        @pl.when((flags & 2) == 0)
        def _():
            acc_scr[...] = acc_scr[...] + rows

    @pl.when((flags & 4) != 0)
    def _():
        o_ref[0] = acc_scr[...].astype(o_ref.dtype)


def _gather(jt, tt, fl, pose, h2, cap, sb, tb, kp):
    n = h2.shape[0]
    nt = n // tb
    pos3 = pose.reshape(N_EXPERTS * nt, 1, tb)
    grid_spec = pltpu.PrefetchScalarGridSpec(
        num_scalar_prefetch=3,
        grid=(N_EXPERTS, kp),
        in_specs=[
            pl.BlockSpec((1, 1, tb), lambda e, k, jt, tt, fl: (e * nt + tt[e * kp + k], 0, 0)),
            pl.BlockSpec((tb, D), lambda e, k, jt, tt, fl: (tt[e * kp + k], 0)),
        ],
        out_specs=pl.BlockSpec((1, sb, D), lambda e, k, jt, tt, fl: (e, jt[e * kp + k], 0)),
        scratch_shapes=[pltpu.VMEM((sb, D), F32)],
    )
    return pl.pallas_call(
        functools.partial(_gather_kernel, sb=sb, kp=kp),
        grid_spec=grid_spec,
        out_shape=jax.ShapeDtypeStruct((N_EXPERTS, cap, D), BF16),
        compiler_params=_params(("arbitrary", "arbitrary")),
        name="gather",
    )(jt, tt, fl, pos3, h2)


def _ffn_kernel(*refs, nsets):
    x_refs = refs[:nsets]
    w1_ref, w3_ref, w2_ref = refs[nsets:nsets + 3]
    o_refs = refs[nsets + 3:2 * nsets + 3]
    acc_refs = refs[2 * nsets + 3:]
    f = pl.program_id(1)
    nf = pl.num_programs(1)

    @pl.when(f == 0)
    def _():
        for acc in acc_refs:
            acc[...] = jnp.zeros(acc.shape, F32)

    w1 = w1_ref[0, 0].astype(BF16)
    w3 = w3_ref[0, 0].astype(BF16)
    w2 = w2_ref[0, 0].astype(BF16)
    for x_ref, acc in zip(x_refs, acc_refs):
        cap = acc.shape[0]
        rb = min(cap, 512)
        for r in range(cap // rb):
            x = x_ref[0, r * rb:(r + 1) * rb, :]
            he = (_silu(_dot(x, w1)) * _dot(x, w3)).astype(BF16)
            acc[r * rb:(r + 1) * rb, :] = acc[r * rb:(r + 1) * rb, :] + _dot(he, w2)

    @pl.when(f == nf - 1)
    def _():
        for o_ref, acc in zip(o_refs, acc_refs):
            o_ref[0] = acc[...].astype(o_ref.dtype)


def _ffn(xes, w1, w3, w2, l):
    fc = 256

    def rows(xe):
        return pl.BlockSpec((1, xe.shape[1], D), lambda e, f: (e, 0, 0))

    return pl.pallas_call(
        functools.partial(_ffn_kernel, nsets=len(xes)),
        grid=(N_EXPERTS, EXPERT_FF // fc),
        in_specs=[rows(xe) for xe in xes] + [
            pl.BlockSpec((1, 1, D, fc), lambda e, f: (l, e, 0, f)),
            pl.BlockSpec((1, 1, D, fc), lambda e, f: (l, e, 0, f)),
            pl.BlockSpec((1, 1, fc, D), lambda e, f: (l, e, f, 0)),
        ],
        out_specs=[rows(xe) for xe in xes],
        out_shape=[jax.ShapeDtypeStruct(xe.shape, BF16) for xe in xes],
        scratch_shapes=[pltpu.VMEM((xe.shape[1], D), F32) for xe in xes],
        compiler_params=_params(("arbitrary", "arbitrary")),
        name="ffn",
    )(*xes, w1, w3, w2)


def _combine_kernel(w0_ref, cnt_ref, pos_ref, gate_ref, x_ref, m5_ref, fw_ref, ye_hbm, o_ref, ybuf, xbuf, acc_scr, sem,
                    *, cap, win, final):
    b = pl.program_id(0)
    nblk = pl.num_programs(0)
    tb = pos_ref.shape[0]
    slot = lax.broadcasted_iota(I32, (tb, win), 1).astype(F32)
    half = b % 2

    def window(blk, e, wb):
        lo = (w0_ref[blk * N_EXPERTS + e] // BF16_ROWS) * BF16_ROWS + wb * win
        start = pl.multiple_of(jnp.minimum(lo, cap - win), BF16_ROWS)
        return lo, start

    def fetch(e, start, buf, s):
        return pltpu.make_async_copy(ye_hbm.at[e, pl.ds(start, win), :], buf, s)

    def first_window(blk, h, e):
        return fetch(e, window(blk, e, 0)[1], ybuf.at[h, e], sem.at[h * N_EXPERTS + e])

    @pl.when(b == 0)
    def _():
        for e in range(N_EXPERTS):
            first_window(0, 0, e).start()

    @pl.when(b + 1 < nblk)
    def _():
        for e in range(N_EXPERTS):
            first_window(b + 1, 1 - half, e).start()

    for e in range(N_EXPERTS):
        first_window(b, half, e).wait()
    acc = jnp.zeros((tb, D), F32)
    for e in range(N_EXPERTS):
        start = window(b, e, 0)[1]
        onehot = jnp.where(pos_ref[:, e:e + 1] - start.astype(F32) == slot, 1.0, 0.0).astype(BF16)
        acc = acc + _dot(onehot, ybuf[half, e]) * gate_ref[:, e:e + 1]
    acc_scr[...] = acc

    for e in range(N_EXPERTS):
        w0 = w0_ref[b * N_EXPERTS + e]
        n_win = (w0 % BF16_ROWS + cnt_ref[b * N_EXPERTS + e] + win - 1) // win

        def extra(wb, carry, e=e):
            lo, start = window(b, e, wb)
            cp = fetch(e, start, xbuf, sem.at[2 * N_EXPERTS])
            cp.start()
            cp.wait()
            posc = pos_ref[:, e:e + 1]
            hit = jnp.where(posc - start.astype(F32) == slot, 1.0, 0.0) * jnp.where(posc >= lo.astype(F32), 1.0, 0.0)
            acc_scr[...] = acc_scr[...] + _dot(hit.astype(BF16), xbuf[...]) * gate_ref[:, e:e + 1]
            return carry

        lax.fori_loop(1, n_win, extra, 0)

    y = x_ref[...] + m5_ref[...] * acc_scr[...]
    if final:
        y = y * lax.rsqrt(jnp.mean(y * y, axis=-1, keepdims=True) + EPS) * fw_ref[...]
    o_ref[...] = y


def _combine(w0, cnt, post, aff, ye, x, m5, fw, tb, final):
    n = x.shape[0]
    cap = ye.shape[1]
    win = min(COMBINE_WIN, cap)
    vec = pl.BlockSpec((1, D), lambda b, w0, cnt: (0, 0))
    grid_spec = pltpu.PrefetchScalarGridSpec(
        num_scalar_prefetch=2,
        grid=(n // tb,),
        in_specs=[
            pl.BlockSpec((tb, LANES), lambda b, w0, cnt: (b, 0)),
            pl.BlockSpec((tb, LANES), lambda b, w0, cnt: (b, 0)),
            pl.BlockSpec((tb, D), lambda b, w0, cnt: (b, 0)),
            vec, vec,
            pl.BlockSpec(memory_space=pl.ANY),
        ],
        out_specs=pl.BlockSpec((tb, D), lambda b, w0, cnt: (b, 0)),
        scratch_shapes=[pltpu.VMEM((2, N_EXPERTS, win, D), BF16), pltpu.VMEM((win, D), BF16),
                        pltpu.VMEM((tb, D), F32), pltpu.SemaphoreType.DMA((2 * N_EXPERTS + 1,))],
    )
    return pl.pallas_call(
        functools.partial(_combine_kernel, cap=cap, win=win, final=final),
        grid_spec=grid_spec,
        out_shape=jax.ShapeDtypeStruct((n, D), F32),
        compiler_params=_params(("arbitrary",)),
        name="combine",
    )(w0, cnt, post, aff, x, m5, fw, ye)


def _pair_tables(off_tb, cap, sb, kp):
    ne, nt = off_tb.shape
    nxt = jnp.concatenate([off_tb[:, 1:], jnp.full((ne, 1), cap, I32)], axis=1)
    cnt = nxt - off_tb
    j_lo = off_tb // sb
    j_hi = jnp.where(cnt > 0, (nxt - 1) // sb, j_lo - 1)
    npair = j_hi - j_lo + 1
    starts = jnp.cumsum(npair, axis=1) - npair
    total = jnp.sum(npair, axis=1, keepdims=True)
    k = jnp.arange(kp, dtype=I32)[None, :]
    kk = jnp.minimum(k, total - 1)
    tb_k = jnp.sum((starts[:, None, :] <= kk[:, :, None]).astype(I32), axis=2) - 1
    j_k = jnp.take_along_axis(j_lo, tb_k, axis=1) + kk - jnp.take_along_axis(starts, tb_k, axis=1)
    valid = (k < total).astype(I32)
    prev_j = jnp.concatenate([jnp.full((ne, 1), -1, I32), j_k[:, :-1]], axis=1)
    first = ((j_k != prev_j) & (k < total)).astype(I32)
    next_j = jnp.concatenate([j_k[:, 1:], jnp.full((ne, 1), -1, I32)], axis=1)
    last = (((j_k != next_j) | (k + 1 >= total)) & (k < total)).astype(I32)
    flags = valid + 2 * first + 4 * last
    return j_k.reshape(-1).astype(I32), tb_k.reshape(-1).astype(I32), flags.reshape(-1).astype(I32)


def _moe_route(x, nw, sh, sc, rw):
    n = x.shape[0]
    cap = CAPACITY_FACTOR * n // N_EXPERTS
    sb = min(256, cap)
    tbg = min(1024, n)
    kp = cap // sb + n // tbg
    h2, aff, afft = _router(x, nw, sh, sc, rw)
    pose, post, off = _select(afft, cap)
    off128 = off[:, :, 0].T.astype(I32)
    jt, tt, fl = _pair_tables(off128[:, :: tbg // LANES], cap, sb, kp)
    xe = _gather(jt, tt, fl, pose, h2, cap, sb, tbg, kp)
    return xe, (post, aff, off128)


def _moe_finish(x, ye, route, m5, fw, final):
    post, aff, off128 = route
    n = x.shape[0]
    cap = ye.shape[1]
    tbc = min(256, n)
    off_c = off128[:, :: tbc // LANES]
    cnt_c = jnp.concatenate([off_c[:, 1:], jnp.full((N_EXPERTS, 1), cap, I32)], axis=1) - off_c
    return _combine(off_c.T.reshape(-1), cnt_c.T.reshape(-1), post, aff, ye, x, m5, fw, tbc, final)


def _rope_tables(n_rows):
    row = jnp.repeat(jnp.arange(n_rows, dtype=F32), GRID_W)
    col = jnp.tile(jnp.arange(GRID_W, dtype=F32), n_rows)
    half = DIFF_HEAD_DIM // 2
    freqs = ROPE_BASE ** (-jnp.arange(0, half, 2, dtype=F32) / half)
    ar, ac = row[:, None] * freqs, col[:, None] * freqs
    cr, sr, cc, sn = jnp.cos(ar), jnp.sin(ar), jnp.cos(ac), jnp.sin(ac)
    z = jnp.zeros_like(sr)
    cos = jnp.concatenate([cr, cr, cc, cc], axis=1)
    sa = jnp.concatenate([-sr, z, -sn, z], axis=1)
    sb = jnp.concatenate([z, sr, z, sn], axis=1)
    return tuple(jnp.tile(t, (1, 2)) for t in (cos, sa, sb))


def _identity_tables(n):
    return (jnp.ones((n, LANES), F32), jnp.zeros((n, LANES), F32), jnp.zeros((n, LANES), F32))


def _proj_weight(w_in):
    sizes = (1024, 256, 32, 1024, 1024, 256, 1024, 1024, 2048, 3072)
    idx = np.cumsum((0,) + sizes)
    px, pb, pdt, pk, pv, pc, pq, pz, pglu, pgate = [w_in[:, idx[i]:idx[i + 1]] for i in range(10)]
    pad = jnp.zeros((D, LANES - SSD_HEADS), w_in.dtype)
    tail = jnp.zeros((D, N_PROJ_F32 - OFF_DT - 2 * LANES), w_in.dtype)
    w = jnp.concatenate([px, pz, pglu, pgate, pb, pc, pdt[:, :SSD_HEADS], pad, pdt[:, SSD_HEADS:], pad, tail,
                         pk, pv, pq], axis=1)
    return w.astype(BF16)


def _lane_pad(v):
    return jnp.pad(v, ((0, 0), (0, LANES - v.shape[1]))).reshape(2, 1, LANES)


def kernel(x, c, ctx, c_ctx, ada_w, ada_b, norm1_w, norm2_w, w_in, ssd_conv_w, ssd_conv_b, ssd_dt_bias, ssd_a_log, ssd_d, ssd_norm_w, ssd_out, diff_lambda, diff_subln_w, diff_out, conf_dw_w, conf_dw_b, conf_ln_w, conf_ln_b, conf_out, w_o, router_w, exp_w1, exp_w3, exp_w2, final_norm_w):
    depth = ada_w.shape[0]
    n = x.shape[1]
    nctx = ctx.shape[1]
    x_lat = x[0]
    x_ctx = ctx[0]

    cc = jnp.zeros((8, D), F32).at[0].set(c[0]).at[1].set(c_ctx)
    mods = _ada(cc, ada_w, ada_b)

    rope = _rope_tables(n // GRID_W)
    ident_ctx = _identity_tables(nctx)
    scale = DIFF_HEAD_DIM ** -0.5 * math.log2(math.e)
    zero_state = jnp.zeros((2, SSD_STATE, SSD_HEADS * SSD_HEAD_DIM), F32)
    fw = final_norm_w.reshape(1, D)

    for l in range(depth):
        last = l == depth - 1
        lam_init = 0.8 - 0.6 * math.exp(-0.3 * l)
        lq1, lk1, lq2, lk2 = diff_lambda[l].astype(F32)
        lam = (jnp.exp(jnp.sum(lq1 * lk1)) - jnp.exp(jnp.sum(lq2 * lk2)) + lam_init).reshape(1)
        m_lat = [mods[l, 0:1, i * D:(i + 1) * D] for i in range(N_MOD)]
        m_ctx = [mods[l, 1:2, i * D:(i + 1) * D] for i in range(N_MOD)]
        nw1 = norm1_w[l].reshape(1, D)
        nw2 = norm2_w[l].reshape(1, D)
        w = _proj_weight(w_in[l])
        dtb = _lane_pad(ssd_dt_bias[l])
        alog = _lane_pad(ssd_a_log[l])
        dsk = jnp.repeat(ssd_d[l], SSD_HEAD_DIM).reshape(1, D)
        gnw = ssd_norm_w[l].reshape(1, D)
        sw = diff_subln_w[l].reshape(1, LANES)
        wa, wb, wc, wo = (t[l].astype(BF16) for t in (ssd_out, diff_out, conf_out, w_o))
        rw = jnp.pad(router_w[l], ((0, 0), (0, LANES - N_EXPERTS))).astype(BF16)
        coef = 1.0 - lam_init

        k_all = jnp.zeros((n + nctx, D), BF16)
        v_all = jnp.zeros((n + nctx, D), BF16)
        p_l, k_all, v_all, q_l = _inproj(x_lat, nw1, m_lat[0], m_lat[1], w, rope, scale, k_all, v_all, 0)
        p_c, k_all, v_all, q_c = _inproj(x_ctx, nw1, m_ctx[0], m_ctx[1], w, ident_ctx, scale, k_all, v_all, n // nctx)
        xbc_c = _ssdconv(p_c, ssd_conv_w[l], ssd_conv_b[l])
        xbc_l = _ssdconv(p_l, ssd_conv_w[l], ssd_conv_b[l])
        yf_c, yb_c, st_c = _ssd(xbc_c, p_c, dtb, alog, zero_state)

        if not last:
            on_c = _attention(lam, q_c, k_all, v_all, sw, nctx, n // nctx, coef)
            cn_c = _conformer(p_c, conf_dw_w[l], conf_dw_b[l], conf_ln_w[l], conf_ln_b[l])
            x_ctx = _merge(yf_c, yb_c, xbc_c, p_c, on_c, cn_c, x_ctx, dsk, gnw, m_ctx[2], wa, wb, wc, wo)
            xe_c, route_c = _moe_route(x_ctx, nw2, m_ctx[3], m_ctx[4], rw)

        yf_l, yb_l, _ = _ssd(xbc_l, p_l, dtb, alog, st_c)
        on_l = _attention(lam, q_l, k_all, v_all, sw, nctx + n, 0, coef)
        cn_l = _conformer(p_l, conf_dw_w[l], conf_dw_b[l], conf_ln_w[l], conf_ln_b[l])
        x_lat = _merge(yf_l, yb_l, xbc_l, p_l, on_l, cn_l, x_lat, dsk, gnw, m_lat[2], wa, wb, wc, wo)
        xe_l, route_l = _moe_route(x_lat, nw2, m_lat[3], m_lat[4], rw)

        if last:
            (ye_l,) = _ffn([xe_l], exp_w1, exp_w3, exp_w2, l)
        else:
            ye_l, ye_c = _ffn([xe_l, xe_c], exp_w1, exp_w3, exp_w2, l)
            x_ctx = _moe_finish(x_ctx, ye_c, route_c, m_ctx[5], fw, False)
        x_lat = _moe_finish(x_lat, ye_l, route_l, m_lat[5], fw, last)

    return x_lat[None]
```

```python
import functools
import math

import jax
import jax.numpy as jnp
import numpy as np
from jax import lax
from jax.experimental import pallas as pl
from jax.experimental.pallas import tpu as pltpu

F32 = jnp.float32
BF16 = jnp.bfloat16
I32 = jnp.int32

EPS = 1e-6
D = 1024
N_MOD = 6
GRID_W = 64
SSD_HEADS = 16
SSD_HEAD_DIM = 64
SSD_STATE = 128
SSD_GN = 256
SSD_CHUNK = 128
DIFF_HEADS = 8
DIFF_HEAD_DIM = 64
ROPE_BASE = 10000.0
CONF_KERNEL = 31
CONF_HALO = 16
N_EXPERTS = 16
EXPERT_FF = 2048
CAPACITY_FACTOR = 2
LANES = 128
BF16_ROWS = 16
COMBINE_WIN = 128

OFF_X, OFF_Z, OFF_GLU_A, OFF_GLU_G, OFF_GATE = 0, 1024, 2048, 3072, 4096
OFF_B, OFF_C, OFF_DT = 7168, 7424, 7680
N_PROJ_F32 = 8192
OFF_K, OFF_V, OFF_Q = 8192, 9216, 10240
N_PROJ = 11264

VMEM_LIMIT = 56 * 1024 * 1024


def _params(sem, vmem=VMEM_LIMIT):
    return pltpu.CompilerParams(dimension_semantics=sem, vmem_limit_bytes=vmem)


def _silu(x):
    return x * jax.nn.sigmoid(x)


def _split3(x):
    a1 = x.astype(BF16)
    r1 = x - a1.astype(F32)
    a2 = r1.astype(BF16)
    a3 = (r1 - a2.astype(F32)).astype(BF16)
    return a1, a2, a3


def _dot(a, b):
    return jnp.dot(a, b, preferred_element_type=F32)


def _ada_kernel(c_ref, w_ref, b_ref, o_ref):
    a = _silu(c_ref[...]).astype(BF16)
    o_ref[0] = _dot(a, w_ref[0].astype(BF16)) + b_ref[0]


def _ada(cc, ada_w, ada_b):
    nl = ada_w.shape[0]
    tn = 1536
    return pl.pallas_call(
        _ada_kernel,
        grid=(nl, N_MOD * D // tn),
        in_specs=[
            pl.BlockSpec((8, D), lambda l, j: (0, 0)),
            pl.BlockSpec((1, D, tn), lambda l, j: (l, 0, j)),
            pl.BlockSpec((1, 1, tn), lambda l, j: (l, 0, j)),
        ],
        out_specs=pl.BlockSpec((1, 8, tn), lambda l, j: (l, 0, j)),
        out_shape=jax.ShapeDtypeStruct((nl, 8, N_MOD * D), F32),
        compiler_params=_params(("arbitrary", "arbitrary")),
        name="ada",
    )(cc, ada_w, ada_b.reshape(nl, 1, N_MOD * D))


def _modulate(x, nw, sh, sc):
    r = lax.rsqrt(jnp.mean(x * x, axis=-1, keepdims=True) + EPS)
    return (x * r * nw) * (1.0 + sc) + sh


def _rope(y, cos, sa, sb):
    parts = []
    for j in range(D // LANES):
        yj = y[:, j * LANES:(j + 1) * LANES]
        parts.append(yj * cos + pltpu.roll(yj, LANES - 16, 1) * sa + pltpu.roll(yj, 16, 1) * sb)
    return jnp.concatenate(parts, axis=1)


def _inproj_kernel(x_ref, nw_ref, sh_ref, sc_ref, w_ref, cos_ref, sa_ref, sb_ref, k_in, v_in,
                   p_ref, k_ref, v_ref, q_ref, h_scr, *, scale):
    del k_in, v_in
    j = pl.program_id(1)
    nf = N_PROJ_F32 // D

    @pl.when(j == 0)
    def _():
        h_scr[...] = _modulate(x_ref[...], nw_ref[...], sh_ref[...], sc_ref[...]).astype(BF16)

    y = _dot(h_scr[...], w_ref[...])

    @pl.when(j < nf)
    def _():
        p_ref[...] = y

    @pl.when(j == nf)
    def _():
        k_ref[...] = _rope(y, cos_ref[...], sa_ref[...], sb_ref[...]).astype(BF16)

    @pl.when(j == nf + 1)
    def _():
        v_ref[...] = y.astype(BF16)

    @pl.when(j == nf + 2)
    def _():
        q_ref[...] = (_rope(y, cos_ref[...], sa_ref[...], sb_ref[...]) * scale).astype(BF16)


def _inproj(x, nw, sh, sc, w, tabs, scale, k_all, v_all, row_block):
    n = x.shape[0]
    tm = min(n, 1024)
    tn = D
    nf = N_PROJ_F32 // tn
    vec = pl.BlockSpec((1, D), lambda i, j: (0, 0))
    tab = pl.BlockSpec((tm, LANES), lambda i, j: (i, 0))
    kv_spec = pl.BlockSpec((tm, D), lambda i, j: (row_block + i, 0))
    kv_shape = jax.ShapeDtypeStruct(k_all.shape, BF16)
    hbm = pl.BlockSpec(memory_space=pl.ANY)
    return pl.pallas_call(
        functools.partial(_inproj_kernel, scale=scale),
        grid=(n // tm, N_PROJ // tn),
        in_specs=[pl.BlockSpec((tm, D), lambda i, j: (i, 0)), vec, vec, vec,
                  pl.BlockSpec((D, tn), lambda i, j: (0, j)), tab, tab, tab, hbm, hbm],
        out_specs=[pl.BlockSpec((tm, tn), lambda i, j: (i, jnp.minimum(j, nf - 1))), kv_spec, kv_spec,
                   pl.BlockSpec((tm, D), lambda i, j: (i, 0))],
        out_shape=[jax.ShapeDtypeStruct((n, N_PROJ_F32), F32), kv_shape, kv_shape,
                   jax.ShapeDtypeStruct((n, D), BF16)],
        scratch_shapes=[pltpu.VMEM((tm, D), BF16)],
        input_output_aliases={8: 1, 9: 2},
        compiler_params=_params(("arbitrary", "arbitrary")),
        name="inproj",
    )(x, nw, sh, sc, w, *tabs, k_all, v_all)


def _ssdconv_kernel(prev_ref, cur_ref, next_ref, w_ref, b_ref, o_ref):
    i = pl.program_id(0)
    nt = pl.num_programs(0)
    u = cur_ref[...]
    t = u.shape[0]
    row = lax.broadcasted_iota(I32, u.shape, 0)
    before = jnp.where(i > 0, prev_ref[7:8, :], 0.0)
    after = jnp.where(i < nt - 1, next_ref[0:1, :], 0.0)
    um1 = jnp.where(row == 0, before, pltpu.roll(u, 1, 0))
    up1 = jnp.where(row == t - 1, after, pltpu.roll(u, t - 1, 0))
    y = um1 * w_ref[0:1, :] + u * w_ref[1:2, :] + up1 * w_ref[2:3, :] + b_ref[...]
    o_ref[...] = _silu(y)


def _ssdconv(p, conv_w, conv_b):
    n = p.shape[0]
    t = min(n, 1024)
    cw = 512
    r8 = t // 8
    nb8 = n // 8

    def col(c):
        return jnp.where(c < 2, c, OFF_B // cw)

    return pl.pallas_call(
        _ssdconv_kernel,
        grid=(n // t, 3),
        in_specs=[
            pl.BlockSpec((8, cw), lambda i, c: (jnp.maximum(i * r8 - 1, 0), col(c))),
            pl.BlockSpec((t, cw), lambda i, c: (i, col(c))),
            pl.BlockSpec((8, cw), lambda i, c: (jnp.minimum((i + 1) * r8, nb8 - 1), col(c))),
            pl.BlockSpec((3, cw), lambda i, c: (0, c)),
            pl.BlockSpec((1, cw), lambda i, c: (0, c)),
        ],
        out_specs=pl.BlockSpec((t, cw), lambda i, c: (i, c)),
        out_shape=jax.ShapeDtypeStruct((n, 3 * cw), F32),
        compiler_params=_params(("arbitrary", "arbitrary")),
        name="ssdconv",
    )(p, p, p, conv_w, conv_b.reshape(1, -1))


def _ssd_direction(d, xs_ref, b_ref, c_ref, dt_ref, dtb, alog, st_ref, y_ref):
    L = SSD_CHUNK
    P = SSD_HEAD_DIM
    HP = SSD_HEADS * P
    GW = HP // 2

    x = dt_ref[...] + dtb
    dt = jnp.maximum(x, 0.0) + jnp.log1p(jnp.exp(-jnp.abs(x)))
    a = dt * (-jnp.exp(alog))

    li = lax.broadcasted_iota(I32, (L, L), 0)
    si = lax.broadcasted_iota(I32, (L, L), 1)
    keep = si <= li if d == 0 else si >= li
    tri = jnp.where(keep, 1.0, 0.0).astype(BF16)
    a1, a2, a3 = _split3(a)
    cum = _dot(tri, a1) + _dot(tri, a2) + _dot(tri, a3)
    cum_t = cum.T

    ex = jnp.where(lax.broadcasted_iota(I32, (LANES, HP), 0) == lax.broadcasted_iota(I32, (LANES, HP), 1) // P,
                   1.0, 0.0).astype(BF16)

    def expand(v):
        v1, v2, v3 = _split3(v)
        return _dot(v1, ex) + _dot(v2, ex) + _dot(v3, ex)

    cum_e = expand(cum)
    dt_e = expand(dt)
    tot_e = cum_e[L - 1:L, :] if d == 0 else cum_e[0:1, :]

    xdt = xs_ref[...] * dt_e
    xb = xdt.astype(BF16)
    xw = (xdt * jnp.exp(tot_e - cum_e)).astype(BF16)
    bb = b_ref[...].astype(BF16)
    cb = c_ref[...].astype(BF16)
    st = st_ref[...]
    stb = st.astype(BF16)
    grow = jnp.exp(cum_e)

    for g in range(2):
        bg = bb[:, g * SSD_STATE:(g + 1) * SSD_STATE]
        cg = cb[:, g * SSD_STATE:(g + 1) * SSD_STATE]
        scores = lax.dot_general(cg, bg, (((1,), (1,)), ((), ())), preferred_element_type=F32)
        y_off = _dot(cg, stb[:, g * GW:(g + 1) * GW]) * grow[:, g * GW:(g + 1) * GW]
        for hh in range(SSD_HEADS // 2):
            h = g * (SSD_HEADS // 2) + hh
            seg = cum[:, h:h + 1] - cum_t[h:h + 1, :]
            dec = jnp.where(keep, jnp.exp(jnp.where(keep, seg, 0.0)), 0.0)
            m = (scores * dec).astype(BF16)
            yd = _dot(m, xb[:, h * P:(h + 1) * P])
            y_ref[:, h * P:(h + 1) * P] = yd + y_off[:, hh * P:(hh + 1) * P]
        upd = lax.dot_general(bg, xw[:, g * GW:(g + 1) * GW], (((0,), (0,)), ((), ())),
                              preferred_element_type=F32)
        st_ref[:, g * GW:(g + 1) * GW] = st[:, g * GW:(g + 1) * GW] * jnp.exp(tot_e[:, g * GW:(g + 1) * GW]) + upd


def _ssd_kernel(xsf_ref, bf_ref, cf_ref, dtf_ref, xsb_ref, bb_ref, cb_ref, dtb_ref, bias_ref, alog_ref, h0_ref,
                yf_ref, yb_ref, hT_ref, st_scr):
    c = pl.program_id(0)

    @pl.when(c == 0)
    def _():
        st_scr[...] = h0_ref[...]

    L = SSD_CHUNK
    nsub = xsf_ref.shape[0] // L
    for k in range(nsub):
        f = slice(k * L, (k + 1) * L)
        r = slice((nsub - 1 - k) * L, (nsub - k) * L)
        _ssd_direction(0, xsf_ref.at[f], bf_ref.at[f], cf_ref.at[f], dtf_ref.at[f], bias_ref[0], alog_ref[0],
                       st_scr.at[0], yf_ref.at[f])
        _ssd_direction(1, xsb_ref.at[r], bb_ref.at[r], cb_ref.at[r], dtb_ref.at[r], bias_ref[1], alog_ref[1],
                       st_scr.at[1], yb_ref.at[r])

    @pl.when(c == pl.num_programs(0) - 1)
    def _():
        hT_ref[...] = st_scr[...]


def _ssd(xbc, p, dtb, alog, h0):
    n = xbc.shape[0]
    L = SSD_CHUNK * (2 if n % (2 * SSD_CHUNK) == 0 else 1)
    nc = n // L
    HP = SSD_HEADS * SSD_HEAD_DIM
    full = pl.BlockSpec((2, SSD_STATE, HP), lambda c: (0, 0, 0))
    vec = pl.BlockSpec((2, 1, LANES), lambda c: (0, 0, 0))

    def side(chunk, d):
        return [
            pl.BlockSpec((L, HP), lambda c: (chunk(c), 0)),
            pl.BlockSpec((L, SSD_GN), lambda c: (chunk(c), HP // SSD_GN)),
            pl.BlockSpec((L, SSD_GN), lambda c: (chunk(c), HP // SSD_GN + 1)),
            pl.BlockSpec((L, LANES), lambda c: (chunk(c), OFF_DT // LANES + d)),
        ]

    def fwd(c):
        return c

    def bwd(c):
        return nc - 1 - c

    return pl.pallas_call(
        _ssd_kernel,
        grid=(nc,),
        in_specs=side(fwd, 0) + side(bwd, 1) + [vec, vec, full],
        out_specs=[pl.BlockSpec((L, HP), lambda c: (c, 0)), pl.BlockSpec((L, HP), lambda c: (nc - 1 - c, 0)), full],
        out_shape=[jax.ShapeDtypeStruct((n, HP), F32), jax.ShapeDtypeStruct((n, HP), F32),
                   jax.ShapeDtypeStruct((2, SSD_STATE, HP), F32)],
        scratch_shapes=[pltpu.VMEM((2, SSD_STATE, HP), F32)],
        compiler_params=_params(("arbitrary",)),
        name="ssd",
    )(xbc, xbc, xbc, p, xbc, xbc, xbc, p, dtb, alog, h0)


def _attn_kernel(lam_ref, q_ref, k_ref, v_ref, sw_ref, o_ref, q2_scr, sa_scr, sb_scr, p_scr, m_scr, acc_scr,
                 *, tq, nq, ck, nchunks, coef):
    lane = lax.broadcasted_iota(I32, (tq, LANES), 1)
    for t in range(nq):
        q = q_ref[t * tq:(t + 1) * tq, :]
        zero = jnp.zeros_like(q)
        q2_scr[t, 0:tq, :] = jnp.where(lane < DIFF_HEAD_DIM, q, zero)
        q2_scr[t, tq:2 * tq, :] = jnp.where(lane >= DIFF_HEAD_DIM, q, zero)
    m_scr[...] = jnp.full(m_scr.shape, -jnp.inf, F32)
    acc_scr[...] = jnp.zeros(acc_scr.shape, F32)
    ones = jnp.ones((ck, LANES), BF16)
    nb = ck // LANES

    def key_rows(c):
        return pl.ds(c * ck, ck) if isinstance(c, int) else pl.ds(pl.multiple_of(c * ck, ck), ck)

    def scores(t, c, s_scr):
        s_scr[...] = lax.dot_general(q2_scr[t], k_ref[key_rows(c), :], (((1,), (1,)), ((), ())),
                                     preferred_element_type=F32)

    def soft_pv(t, c, s_scr):
        rows = key_rows(c)
        mx = s_scr[:, 0:LANES]
        for b in range(1, nb):
            mx = jnp.maximum(mx, s_scr[:, b * LANES:(b + 1) * LANES])
        m_prev = m_scr[t]
        m_new = jnp.maximum(m_prev, jnp.max(mx, axis=1, keepdims=True))
        alpha = jnp.exp2(m_prev - m_new)
        for b in range(nb):
            p_scr[:, b * LANES:(b + 1) * LANES] = jnp.exp2(s_scr[:, b * LANES:(b + 1) * LANES] - m_new).astype(BF16)
        va = jnp.concatenate([v_ref[rows, :], ones], axis=1)
        acc_scr[t] = acc_scr[t] * jnp.concatenate([alpha, alpha], axis=1) + _dot(p_scr[...], va)
        m_scr[t] = m_new

    def tile(t, cur, nxt):
        def pair(jj, carry):
            scores(t, 2 * jj + 1, nxt)
            soft_pv(t, 2 * jj, cur)
            scores(t, 2 * jj + 2, cur)
            soft_pv(t, 2 * jj + 1, nxt)
            return carry

        lax.fori_loop(0, (nchunks - 1) // 2, pair, 0)
        last, free = cur, nxt
        if (nchunks - 1) % 2 == 1:
            scores(t, nchunks - 1, nxt)
            soft_pv(t, nchunks - 2, cur)
            last, free = nxt, cur
        if t + 1 < nq:
            scores(t + 1, 0, free)
        soft_pv(t, nchunks - 1, last)
        return free, last

    scores(0, 0, sa_scr)
    bufs = (sa_scr, sb_scr)
    for t in range(nq):
        bufs = tile(t, *bufs)

    for t in range(nq):
        on = acc_scr[t, :, 0:LANES] / acc_scr[t, :, LANES:2 * LANES]
        o = on[0:tq, :] - lam_ref[0] * on[tq:2 * tq, :]
        r = lax.rsqrt(jnp.mean(o * o, axis=-1, keepdims=True) + EPS)
        o_ref[t * tq:(t + 1) * tq, :] = ((o * r * sw_ref[...]) * coef).astype(o_ref.dtype)


def _pick_tk(m):
    for cand in (1280, 1024, 512, 256):
        if m % cand == 0:
            return cand
    return m


def _attention(lam, q, k, v, subln_w, m, key_block, coef):
    n = q.shape[0]
    tq = min(n, 512)
    nq = min(2, n // tq)
    ck = _pick_tk(m)
    return pl.pallas_call(
        functools.partial(_attn_kernel, tq=tq, nq=nq, ck=ck, nchunks=m // ck, coef=coef),
        grid=(DIFF_HEADS, n // (nq * tq)),
        in_specs=[
            pl.BlockSpec(memory_space=pltpu.SMEM),
            pl.BlockSpec((nq * tq, LANES), lambda h, i: (i, h)),
            pl.BlockSpec((m, LANES), lambda h, i: (key_block, h)),
            pl.BlockSpec((m, LANES), lambda h, i: (key_block, h)),
            pl.BlockSpec((1, LANES), lambda h, i: (0, 0)),
        ],
        out_specs=pl.BlockSpec((nq * tq, LANES), lambda h, i: (i, h)),
        out_shape=jax.ShapeDtypeStruct((n, D), BF16),
        scratch_shapes=[pltpu.VMEM((nq, 2 * tq, LANES), BF16), pltpu.VMEM((2 * tq, ck), F32),
                        pltpu.VMEM((2 * tq, ck), F32), pltpu.VMEM((2 * tq, ck), BF16),
                        pltpu.VMEM((nq, 2 * tq, LANES), F32), pltpu.VMEM((nq, 2 * tq, 2 * LANES), F32)],
        compiler_params=_params(("arbitrary", "arbitrary")),
        name="attn",
    )(lam, q, k, v, subln_w)


def _conf_kernel(ap_ref, gp_ref, a_ref, g_ref, an_ref, gn_ref, w_ref, b_ref, lw_ref, lb_ref, o_ref, ext_scr, sh_scr):
    i = pl.program_id(0)
    nt = pl.num_programs(0)
    t = a_ref.shape[0]
    hl = CONF_HALO

    def glu(a, g):
        return a[...] * jax.nn.sigmoid(g[...])

    ext_scr[0:hl, :] = jnp.where(i > 0, glu(ap_ref, gp_ref), 0.0)
    ext_scr[hl:hl + t, :] = glu(a_ref, g_ref)
    ext_scr[hl + t:2 * hl + t, :] = jnp.where(i < nt - 1, glu(an_ref, gn_ref), 0.0)

    span = t + 2 * hl - 8
    for r in range(1, 8):
        sh_scr[r, 0:span, :] = ext_scr[pl.ds(r, span), :]
    acc = jnp.zeros((t, D), F32) + b_ref[...]
    for k in range(CONF_KERNEL):
        off = k + hl - CONF_KERNEL // 2
        src = ext_scr if off % 8 == 0 else sh_scr.at[off % 8]
        acc = acc + src[pl.ds(off - off % 8, t), :] * w_ref[k:k + 1, :]
    mu = jnp.mean(acc, axis=-1, keepdims=True)
    xc = acc - mu
    y = xc * lax.rsqrt(jnp.mean(xc * xc, axis=-1, keepdims=True) + EPS) * lw_ref[...] + lb_ref[...]
    o_ref[...] = _silu(y).astype(o_ref.dtype)


def _conformer(p, dw_w, dw_b, ln_w, ln_b):
    n = p.shape[0]
    t = min(n, 512)
    hl = CONF_HALO
    rh = t // hl
    nbh = n // hl
    ca, cg = OFF_GLU_A // D, OFF_GLU_G // D

    def prev(i):
        return jnp.maximum(i * rh - 1, 0)

    def nxt(i):
        return jnp.minimum((i + 1) * rh, nbh - 1)

    vec = pl.BlockSpec((1, D), lambda i: (0, 0))
    return pl.pallas_call(
        _conf_kernel,
        grid=(n // t,),
        in_specs=[
            pl.BlockSpec((hl, D), lambda i: (prev(i), ca)), pl.BlockSpec((hl, D), lambda i: (prev(i), cg)),
            pl.BlockSpec((t, D), lambda i: (i, ca)), pl.BlockSpec((t, D), lambda i: (i, cg)),
            pl.BlockSpec((hl, D), lambda i: (nxt(i), ca)), pl.BlockSpec((hl, D), lambda i: (nxt(i), cg)),
            pl.BlockSpec((CONF_KERNEL, D), lambda i: (0, 0)), vec, vec, vec,
        ],
        out_specs=pl.BlockSpec((t, D), lambda i: (i, 0)),
        out_shape=jax.ShapeDtypeStruct((n, D), BF16),
        scratch_shapes=[pltpu.VMEM((t + 2 * hl, D), F32), pltpu.VMEM((8, t + 2 * hl, D), F32)],
        compiler_params=_params(("arbitrary",)),
        name="conformer",
    )(p, p, p, p, p, p, dw_w, dw_b.reshape(1, D), ln_w.reshape(1, D), ln_b.reshape(1, D))


def _merge_kernel(yf_ref, yb_ref, xs_ref, z_ref, on_ref, cn_ref, g0_ref, g1_ref, g2_ref, x_ref, dsk_ref, nw_ref,
                  m2_ref, wa_ref, wb_ref, wc_ref, wo_ref, o_ref):
    y = yf_ref[...] + yb_ref[...] + xs_ref[...] * dsk_ref[...]
    yz = y * _silu(z_ref[...])
    half = D // 2
    parts = []
    for g in range(2):
        seg = yz[:, g * half:(g + 1) * half]
        r = lax.rsqrt(jnp.mean(seg * seg, axis=-1, keepdims=True) + EPS)
        parts.append(seg * r * nw_ref[:, g * half:(g + 1) * half])
    gn = jnp.concatenate(parts, axis=1).astype(BF16)
    y_a = _dot(gn, wa_ref[...])
    y_b = _dot(on_ref[...], wb_ref[...])
    y_c = _dot(cn_ref[...], wc_ref[...])
    m = jax.nn.sigmoid(g0_ref[...]) * y_a + jax.nn.sigmoid(g1_ref[...]) * y_b + jax.nn.sigmoid(g2_ref[...]) * y_c
    y_o = _dot(m.astype(BF16), wo_ref[...])
    o_ref[...] = x_ref[...] + m2_ref[...] * y_o


def _merge(yf, yb, xbc, p, on, cn, x, dsk, nw, m2, wa, wb, wc, wo):
    n = x.shape[0]
    tm = min(n, 256)
    vec = pl.BlockSpec((1, D), lambda i: (0, 0))
    wsp = pl.BlockSpec((D, D), lambda i: (0, 0))
    row = pl.BlockSpec((tm, D), lambda i: (i, 0))
    return pl.pallas_call(
        _merge_kernel,
        grid=(n // tm,),
        in_specs=[
            row, row,
            row,
            pl.BlockSpec((tm, D), lambda i: (i, OFF_Z // D)),
            row, row,
            pl.BlockSpec((tm, D), lambda i: (i, OFF_GATE // D)),
            pl.BlockSpec((tm, D), lambda i: (i, OFF_GATE // D + 1)),
            pl.BlockSpec((tm, D), lambda i: (i, OFF_GATE // D + 2)),
            row, vec, vec, vec, wsp, wsp, wsp, wsp,
        ],
        out_specs=row,
        out_shape=jax.ShapeDtypeStruct((n, D), F32),
        compiler_params=_params(("arbitrary",)),
        name="merge",
    )(yf, yb, xbc, p, on, cn, p, p, p, x, dsk, nw, m2, wa, wb, wc, wo)


def _router_kernel(x_ref, nw_ref, sh_ref, sc_ref, rw_ref, h_ref, aff_ref, afft_ref):
    h = _modulate(x_ref[...], nw_ref[...], sh_ref[...], sc_ref[...]).astype(BF16)
    h_ref[...] = h
    logits = _dot(h, rw_ref[...])
    lane = lax.broadcasted_iota(I32, logits.shape, 1)
    logits = jnp.where(lane < N_EXPERTS, logits, -jnp.inf)
    e = jnp.exp(logits - jnp.max(logits, axis=-1, keepdims=True))
    aff = e / jnp.sum(e, axis=-1, keepdims=True)
    aff_ref[...] = aff
    afft_ref[...] = aff.T[0:N_EXPERTS, :]


def _router(x, nw, sh, sc, rw):
    n = x.shape[0]
    tm = min(n, 512)
    vec = pl.BlockSpec((1, D), lambda i: (0, 0))
    return pl.pallas_call(
        _router_kernel,
        grid=(n // tm,),
        in_specs=[pl.BlockSpec((tm, D), lambda i: (i, 0)), vec, vec, vec, pl.BlockSpec((D, LANES), lambda i: (0, 0))],
        out_specs=[pl.BlockSpec((tm, D), lambda i: (i, 0)), pl.BlockSpec((tm, LANES), lambda i: (i, 0)),
                   pl.BlockSpec((N_EXPERTS, tm), lambda i: (0, i))],
        out_shape=[jax.ShapeDtypeStruct((n, D), BF16), jax.ShapeDtypeStruct((n, LANES), F32),
                   jax.ShapeDtypeStruct((N_EXPERTS, n), F32)],
        compiler_params=_params(("arbitrary",)),
        name="router",
    )(x, nw, sh, sc, rw)


def _select_kernel(a_ref, pose_ref, post_ref, off_ref, *, cap):
    n = a_ref.shape[1]
    nb = n // LANES
    def search(it, thr):
        bits = pltpu.bitcast(a_ref[...], I32)
        cand = thr | lax.shift_left(jnp.int32(1), 30 - it)
        cnt = jnp.sum(jnp.where(bits >= cand, 1.0, 0.0), axis=1, keepdims=True)
        return jnp.where(cnt >= cap, cand, thr)

    thr = lax.fori_loop(0, 31, search, jnp.zeros((N_EXPERTS, 1), I32))
    n_gt = jnp.sum(jnp.where(pltpu.bitcast(a_ref[...], I32) > thr, 1.0, 0.0), axis=1, keepdims=True)
    need = cap - n_gt

    tri = jnp.where(lax.broadcasted_iota(I32, (LANES, LANES), 0) <= lax.broadcasted_iota(I32, (LANES, LANES), 1),
                    1.0, 0.0).astype(BF16)

    def block(b, carry):
        c_eq, c_sel = carry
        start = pl.multiple_of(b * LANES, LANES)
        bb = pltpu.bitcast(a_ref[:, pl.ds(start, LANES)], I32)
        gt = jnp.where(bb > thr, 1.0, 0.0)
        eq = jnp.where(bb == thr, 1.0, 0.0)
        inc_eq = _dot(eq.astype(BF16), tri)
        rank_eq = c_eq + inc_eq - eq
        sel = gt + eq * jnp.where(rank_eq < need, 1.0, 0.0)
        inc_sel = _dot(sel.astype(BF16), tri)
        pos = jnp.where(sel > 0.0, c_sel + inc_sel - sel, -1.0)
        pose_ref[:, pl.ds(start, LANES)] = pos.astype(I32)
        full = jnp.concatenate([pos, jnp.full((LANES - N_EXPERTS, LANES), -1.0, F32)], axis=0)
        post_ref[pl.ds(start, LANES), :] = full.T
        off_ref[b] = jnp.broadcast_to(c_sel, (N_EXPERTS, LANES))
        return c_eq + inc_eq[:, LANES - 1:LANES], c_sel + inc_sel[:, LANES - 1:LANES]

    zero = jnp.zeros((N_EXPERTS, 1), F32)
    lax.fori_loop(0, nb, block, (zero, zero))


def _select(afft, cap):
    n = afft.shape[1]
    nb = n // LANES
    return pl.pallas_call(
        functools.partial(_select_kernel, cap=cap),
        out_shape=[jax.ShapeDtypeStruct((N_EXPERTS, n), I32), jax.ShapeDtypeStruct((n, LANES), F32),
                   jax.ShapeDtypeStruct((nb, N_EXPERTS, LANES), F32)],
        compiler_params=_params(None),
        name="select",
    )(afft)


def _gather_kernel(jt_ref, tt_ref, fl_ref, pos_ref, h_ref, o_ref, acc_scr, *, sb, kp):
    e = pl.program_id(0)
    k = pl.program_id(1)
    idx = e * kp + k
    j = jt_ref[idx]
    flags = fl_ref[idx]

    @pl.when((flags & 1) != 0)
    def _():
        tb = pos_ref.shape[2]
        kb = min(tb, 256)
        slot = lax.broadcasted_iota(I32, (sb, kb), 0) + j * sb
        rows = jnp.zeros((sb, D), F32)
        for t in range(tb // kb):
            onehot = jnp.where(pos_ref[0, :, t * kb:(t + 1) * kb] == slot, 1.0, 0.0).astype(BF16)
            rows = rows + _dot(onehot, h_ref[t * kb:(t + 1) * kb, :])

        @pl.when((flags & 2) != 0)
        def _():
            acc_scr[...] = rows

        @pl.when((flags & 2) == 0)
        def _():
            acc_scr[...] = acc_scr[...] + rows

    @pl.when((flags & 4) != 0)
    def _():
        o_ref[0] = acc_scr[...].astype(o_ref.dtype)


def _gather(jt, tt, fl, pose, h2, cap, sb, tb, kp):
    n = h2.shape[0]
    nt = n // tb
    pos3 = pose.reshape(N_EXPERTS * nt, 1, tb)
    grid_spec = pltpu.PrefetchScalarGridSpec(
        num_scalar_prefetch=3,
        grid=(N_EXPERTS, kp),
        in_specs=[
            pl.BlockSpec((1, 1, tb), lambda e, k, jt, tt, fl: (e * nt + tt[e * kp + k], 0, 0)),
            pl.BlockSpec((tb, D), lambda e, k, jt, tt, fl: (tt[e * kp + k], 0)),
        ],
        out_specs=pl.BlockSpec((1, sb, D), lambda e, k, jt, tt, fl: (e, jt[e * kp + k], 0)),
        scratch_shapes=[pltpu.VMEM((sb, D), F32)],
    )
    return pl.pallas_call(
        functools.partial(_gather_kernel, sb=sb, kp=kp),
        grid_spec=grid_spec,
        out_shape=jax.ShapeDtypeStruct((N_EXPERTS, cap, D), BF16),
        compiler_params=_params(("arbitrary", "arbitrary")),
        name="gather",
    )(jt, tt, fl, pos3, h2)


def _ffn_kernel(*refs, nsets):
    x_refs = refs[:nsets]
    w1_ref, w3_ref, w2_ref = refs[nsets:nsets + 3]
    o_refs = refs[nsets + 3:2 * nsets + 3]
    acc_refs = refs[2 * nsets + 3:]
    f = pl.program_id(1)
    nf = pl.num_programs(1)

    @pl.when(f == 0)
    def _():
        for acc in acc_refs:
            acc[...] = jnp.zeros(acc.shape, F32)

    w1 = w1_ref[0, 0].astype(BF16)
    w3 = w3_ref[0, 0].astype(BF16)
    w2 = w2_ref[0, 0].astype(BF16)
    for x_ref, acc in zip(x_refs, acc_refs):
        cap = acc.shape[0]
        rb = min(cap, 512)
        for r in range(cap // rb):
            x = x_ref[0, r * rb:(r + 1) * rb, :]
            he = (_silu(_dot(x, w1)) * _dot(x, w3)).astype(BF16)
            acc[r * rb:(r + 1) * rb, :] = acc[r * rb:(r + 1) * rb, :] + _dot(he, w2)

    @pl.when(f == nf - 1)
    def _():
        for o_ref, acc in zip(o_refs, acc_refs):
            o_ref[0] = acc[...].astype(o_ref.dtype)


def _ffn(xes, w1, w3, w2, l):
    fc = 256

    def rows(xe):
        return pl.BlockSpec((1, xe.shape[1], D), lambda e, f: (e, 0, 0))

    return pl.pallas_call(
        functools.partial(_ffn_kernel, nsets=len(xes)),
        grid=(N_EXPERTS, EXPERT_FF // fc),
        in_specs=[rows(xe) for xe in xes] + [
            pl.BlockSpec((1, 1, D, fc), lambda e, f: (l, e, 0, f)),
            pl.BlockSpec((1, 1, D, fc), lambda e, f: (l, e, 0, f)),
            pl.BlockSpec((1, 1, fc, D), lambda e, f: (l, e, f, 0)),
        ],
        out_specs=[rows(xe) for xe in xes],
        out_shape=[jax.ShapeDtypeStruct(xe.shape, BF16) for xe in xes],
        scratch_shapes=[pltpu.VMEM((xe.shape[1], D), F32) for xe in xes],
        compiler_params=_params(("arbitrary", "arbitrary")),
        name="ffn",
    )(*xes, w1, w3, w2)


def _combine_kernel(w0_ref, cnt_ref, pos_ref, gate_ref, x_ref, m5_ref, fw_ref, ye_hbm, o_ref, ybuf, xbuf, acc_scr, sem,
                    *, cap, win, final):
    b = pl.program_id(0)
    nblk = pl.num_programs(0)
    tb = pos_ref.shape[0]
    slot = lax.broadcasted_iota(I32, (tb, win), 1).astype(F32)
    half = b % 2

    def window(blk, e, wb):
        lo = (w0_ref[blk * N_EXPERTS + e] // BF16_ROWS) * BF16_ROWS + wb * win
        start = pl.multiple_of(jnp.minimum(lo, cap - win), BF16_ROWS)
        return lo, start

    def fetch(e, start, buf, s):
        return pltpu.make_async_copy(ye_hbm.at[e, pl.ds(start, win), :], buf, s)

    def first_window(blk, h, e):
        return fetch(e, window(blk, e, 0)[1], ybuf.at[h, e], sem.at[h * N_EXPERTS + e])

    @pl.when(b == 0)
    def _():
        for e in range(N_EXPERTS):
            first_window(0, 0, e).start()

    @pl.when(b + 1 < nblk)
    def _():
        for e in range(N_EXPERTS):
            first_window(b + 1, 1 - half, e).start()

    for e in range(N_EXPERTS):
        first_window(b, half, e).wait()
    acc = jnp.zeros((tb, D), F32)
    for e in range(N_EXPERTS):
        start = window(b, e, 0)[1]
        onehot = jnp.where(pos_ref[:, e:e + 1] - start.astype(F32) == slot, 1.0, 0.0).astype(BF16)
        acc = acc + _dot(onehot, ybuf[half, e]) * gate_ref[:, e:e + 1]
    acc_scr[...] = acc

    for e in range(N_EXPERTS):
        w0 = w0_ref[b * N_EXPERTS + e]
        n_win = (w0 % BF16_ROWS + cnt_ref[b * N_EXPERTS + e] + win - 1) // win

        def extra(wb, carry, e=e):
            lo, start = window(b, e, wb)
            cp = fetch(e, start, xbuf, sem.at[2 * N_EXPERTS])
            cp.start()
            cp.wait()
            posc = pos_ref[:, e:e + 1]
            hit = jnp.where(posc - start.astype(F32) == slot, 1.0, 0.0) * jnp.where(posc >= lo.astype(F32), 1.0, 0.0)
            acc_scr[...] = acc_scr[...] + _dot(hit.astype(BF16), xbuf[...]) * gate_ref[:, e:e + 1]
            return carry

        lax.fori_loop(1, n_win, extra, 0)

    y = x_ref[...] + m5_ref[...] * acc_scr[...]
    if final:
        y = y * lax.rsqrt(jnp.mean(y * y, axis=-1, keepdims=True) + EPS) * fw_ref[...]
    o_ref[...] = y


def _combine(w0, cnt, post, aff, ye, x, m5, fw, tb, final):
    n = x.shape[0]
    cap = ye.shape[1]
    win = min(COMBINE_WIN, cap)
    vec = pl.BlockSpec((1, D), lambda b, w0, cnt: (0, 0))
    grid_spec = pltpu.PrefetchScalarGridSpec(
        num_scalar_prefetch=2,
        grid=(n // tb,),
        in_specs=[
            pl.BlockSpec((tb, LANES), lambda b, w0, cnt: (b, 0)),
            pl.BlockSpec((tb, LANES), lambda b, w0, cnt: (b, 0)),
            pl.BlockSpec((tb, D), lambda b, w0, cnt: (b, 0)),
            vec, vec,
            pl.BlockSpec(memory_space=pl.ANY),
        ],
        out_specs=pl.BlockSpec((tb, D), lambda b, w0, cnt: (b, 0)),
        scratch_shapes=[pltpu.VMEM((2, N_EXPERTS, win, D), BF16), pltpu.VMEM((win, D), BF16),
                        pltpu.VMEM((tb, D), F32), pltpu.SemaphoreType.DMA((2 * N_EXPERTS + 1,))],
    )
    return pl.pallas_call(
        functools.partial(_combine_kernel, cap=cap, win=win, final=final),
        grid_spec=grid_spec,
        out_shape=jax.ShapeDtypeStruct((n, D), F32),
        compiler_params=_params(("arbitrary",)),
        name="combine",
    )(w0, cnt, post, aff, x, m5, fw, ye)


def _pair_tables(off_tb, cap, sb, kp):
    ne, nt = off_tb.shape
    nxt = jnp.concatenate([off_tb[:, 1:], jnp.full((ne, 1), cap, I32)], axis=1)
    cnt = nxt - off_tb
    j_lo = off_tb // sb
    j_hi = jnp.where(cnt > 0, (nxt - 1) // sb, j_lo - 1)
    npair = j_hi - j_lo + 1
    starts = jnp.cumsum(npair, axis=1) - npair
    total = jnp.sum(npair, axis=1, keepdims=True)
    k = jnp.arange(kp, dtype=I32)[None, :]
    kk = jnp.minimum(k, total - 1)
    tb_k = jnp.sum((starts[:, None, :] <= kk[:, :, None]).astype(I32), axis=2) - 1
    j_k = jnp.take_along_axis(j_lo, tb_k, axis=1) + kk - jnp.take_along_axis(starts, tb_k, axis=1)
    valid = (k < total).astype(I32)
    prev_j = jnp.concatenate([jnp.full((ne, 1), -1, I32), j_k[:, :-1]], axis=1)
    first = ((j_k != prev_j) & (k < total)).astype(I32)
    next_j = jnp.concatenate([j_k[:, 1:], jnp.full((ne, 1), -1, I32)], axis=1)
    last = (((j_k != next_j) | (k + 1 >= total)) & (k < total)).astype(I32)
    flags = valid + 2 * first + 4 * last
    return j_k.reshape(-1).astype(I32), tb_k.reshape(-1).astype(I32), flags.reshape(-1).astype(I32)


def _moe_route(x, nw, sh, sc, rw):
    n = x.shape[0]
    cap = CAPACITY_FACTOR * n // N_EXPERTS
    sb = min(256, cap)
    tbg = min(1024, n)
    kp = cap // sb + n // tbg
    h2, aff, afft = _router(x, nw, sh, sc, rw)
    pose, post, off = _select(afft, cap)
    off128 = off[:, :, 0].T.astype(I32)
    jt, tt, fl = _pair_tables(off128[:, :: tbg // LANES], cap, sb, kp)
    xe = _gather(jt, tt, fl, pose, h2, cap, sb, tbg, kp)
    return xe, (post, aff, off128)


def _moe_finish(x, ye, route, m5, fw, final):
    post, aff, off128 = route
    n = x.shape[0]
    cap = ye.shape[1]
    tbc = min(256, n)
    off_c = off128[:, :: tbc // LANES]
    cnt_c = jnp.concatenate([off_c[:, 1:], jnp.full((N_EXPERTS, 1), cap, I32)], axis=1) - off_c
    return _combine(off_c.T.reshape(-1), cnt_c.T.reshape(-1), post, aff, ye, x, m5, fw, tbc, final)


def _rope_tables(n_rows):
    row = jnp.repeat(jnp.arange(n_rows, dtype=F32), GRID_W)
    col = jnp.tile(jnp.arange(GRID_W, dtype=F32), n_rows)
    half = DIFF_HEAD_DIM // 2
    freqs = ROPE_BASE ** (-jnp.arange(0, half, 2, dtype=F32) / half)
    ar, ac = row[:, None] * freqs, col[:, None] * freqs
    cr, sr, cc, sn = jnp.cos(ar), jnp.sin(ar), jnp.cos(ac), jnp.sin(ac)
    z = jnp.zeros_like(sr)
    cos = jnp.concatenate([cr, cr, cc, cc], axis=1)
    sa = jnp.concatenate([-sr, z, -sn, z], axis=1)
    sb = jnp.concatenate([z, sr, z, sn], axis=1)
    return tuple(jnp.tile(t, (1, 2)) for t in (cos, sa, sb))


def _identity_tables(n):
    return (jnp.ones((n, LANES), F32), jnp.zeros((n, LANES), F32), jnp.zeros((n, LANES), F32))


def _proj_weight(w_in):
    sizes = (1024, 256, 32, 1024, 1024, 256, 1024, 1024, 2048, 3072)
    idx = np.cumsum((0,) + sizes)
    px, pb, pdt, pk, pv, pc, pq, pz, pglu, pgate = [w_in[:, idx[i]:idx[i + 1]] for i in range(10)]
    pad = jnp.zeros((D, LANES - SSD_HEADS), w_in.dtype)
    tail = jnp.zeros((D, N_PROJ_F32 - OFF_DT - 2 * LANES), w_in.dtype)
    w = jnp.concatenate([px, pz, pglu, pgate, pb, pc, pdt[:, :SSD_HEADS], pad, pdt[:, SSD_HEADS:], pad, tail,
                         pk, pv, pq], axis=1)
    return w.astype(BF16)


def _lane_pad(v):
    return jnp.pad(v, ((0, 0), (0, LANES - v.shape[1]))).reshape(2, 1, LANES)


def kernel(x, c, ctx, c_ctx, ada_w, ada_b, norm1_w, norm2_w, w_in, ssd_conv_w, ssd_conv_b, ssd_dt_bias, ssd_a_log, ssd_d, ssd_norm_w, ssd_out, diff_lambda, diff_subln_w, diff_out, conf_dw_w, conf_dw_b, conf_ln_w, conf_ln_b, conf_out, w_o, router_w, exp_w1, exp_w3, exp_w2, final_norm_w):
    depth = ada_w.shape[0]
    n = x.shape[1]
    nctx = ctx.shape[1]
    x_lat = x[0]
    x_ctx = ctx[0]

    cc = jnp.zeros((8, D), F32).at[0].set(c[0]).at[1].set(c_ctx)
    mods = _ada(cc, ada_w, ada_b)

    rope = _rope_tables(n // GRID_W)
    ident_ctx = _identity_tables(nctx)
    scale = DIFF_HEAD_DIM ** -0.5 * math.log2(math.e)
    zero_state = jnp.zeros((2, SSD_STATE, SSD_HEADS * SSD_HEAD_DIM), F32)
    fw = final_norm_w.reshape(1, D)

    for l in range(depth):
        last = l == depth - 1
        lam_init = 0.8 - 0.6 * math.exp(-0.3 * l)
        lq1, lk1, lq2, lk2 = diff_lambda[l].astype(F32)
        lam = (jnp.exp(jnp.sum(lq1 * lk1)) - jnp.exp(jnp.sum(lq2 * lk2)) + lam_init).reshape(1)
        m_lat = [mods[l, 0:1, i * D:(i + 1) * D] for i in range(N_MOD)]
        m_ctx = [mods[l, 1:2, i * D:(i + 1) * D] for i in range(N_MOD)]
        nw1 = norm1_w[l].reshape(1, D)
        nw2 = norm2_w[l].reshape(1, D)
        w = _proj_weight(w_in[l])
        dtb = _lane_pad(ssd_dt_bias[l])
        alog = _lane_pad(ssd_a_log[l])
        dsk = jnp.repeat(ssd_d[l], SSD_HEAD_DIM).reshape(1, D)
        gnw = ssd_norm_w[l].reshape(1, D)
        sw = diff_subln_w[l].reshape(1, LANES)
        wa, wb, wc, wo = (t[l].astype(BF16) for t in (ssd_out, diff_out, conf_out, w_o))
        rw = jnp.pad(router_w[l], ((0, 0), (0, LANES - N_EXPERTS))).astype(BF16)
        coef = 1.0 - lam_init

        k_all = jnp.zeros((n + nctx, D), BF16)
        v_all = jnp.zeros((n + nctx, D), BF16)
        p_l, k_all, v_all, q_l = _inproj(x_lat, nw1, m_lat[0], m_lat[1], w, rope, scale, k_all, v_all, 0)
        p_c, k_all, v_all, q_c = _inproj(x_ctx, nw1, m_ctx[0], m_ctx[1], w, ident_ctx, scale, k_all, v_all, n // nctx)
        xbc_c = _ssdconv(p_c, ssd_conv_w[l], ssd_conv_b[l])
        xbc_l = _ssdconv(p_l, ssd_conv_w[l], ssd_conv_b[l])
        yf_c, yb_c, st_c = _ssd(xbc_c, p_c, dtb, alog, zero_state)

        if not last:
            on_c = _attention(lam, q_c, k_all, v_all, sw, nctx, n // nctx, coef)
            cn_c = _conformer(p_c, conf_dw_w[l], conf_dw_b[l], conf_ln_w[l], conf_ln_b[l])
            x_ctx = _merge(yf_c, yb_c, xbc_c, p_c, on_c, cn_c, x_ctx, dsk, gnw, m_ctx[2], wa, wb, wc, wo)
            xe_c, route_c = _moe_route(x_ctx, nw2, m_ctx[3], m_ctx[4], rw)

        yf_l, yb_l, _ = _ssd(xbc_l, p_l, dtb, alog, st_c)
        on_l = _attention(lam, q_l, k_all, v_all, sw, nctx + n, 0, coef)
        cn_l = _conformer(p_l, conf_dw_w[l], conf_dw_b[l], conf_ln_w[l], conf_ln_b[l])
        x_lat = _merge(yf_l, yb_l, xbc_l, p_l, on_l, cn_l, x_lat, dsk, gnw, m_lat[2], wa, wb, wc, wo)
        xe_l, route_l = _moe_route(x_lat, nw2, m_lat[3], m_lat[4], rw)

        if last:
            (ye_l,) = _ffn([xe_l], exp_w1, exp_w3, exp_w2, l)
        else:
            ye_l, ye_c = _ffn([xe_l, xe_c], exp_w1, exp_w3, exp_w2, l)
            x_ctx = _moe_finish(x_ctx, ye_c, route_c, m_ctx[5], fw, False)
        x_lat = _moe_finish(x_lat, ye_l, route_l, m_lat[5], fw, last)

    return x_lat[None]
```

```python
import functools
import math

import jax
import jax.numpy as jnp
import numpy as np
from jax import lax
from jax.experimental import pallas as pl
from jax.experimental.pallas import tpu as pltpu

F32 = jnp.float32
BF16 = jnp.bfloat16
I32 = jnp.int32

EPS = 1e-6
D = 1024
N_MOD = 6
GRID_W = 64
SSD_HEADS = 16
SSD_HEAD_DIM = 64
SSD_STATE = 128
SSD_GN = 256
SSD_CHUNK = 128
DIFF_HEADS = 8
DIFF_HEAD_DIM = 64
ROPE_BASE = 10000.0
CONF_KERNEL = 31
CONF_HALO = 16
N_EXPERTS = 16
EXPERT_FF = 2048
CAPACITY_FACTOR = 2
LANES = 128
BF16_ROWS = 16
COMBINE_WIN = 128

OFF_X, OFF_Z, OFF_GLU_A, OFF_GLU_G, OFF_GATE = 0, 1024, 2048, 3072, 4096
OFF_B, OFF_C, OFF_DT = 7168, 7424, 7680
N_PROJ_F32 = 8192
OFF_K, OFF_V, OFF_Q = 8192, 9216, 10240
N_PROJ = 11264

VMEM_LIMIT = 56 * 1024 * 1024


def _params(sem, vmem=VMEM_LIMIT):
    return pltpu.CompilerParams(dimension_semantics=sem, vmem_limit_bytes=vmem)


def _silu(x):
    return x * jax.nn.sigmoid(x)


def _split3(x):
    a1 = x.astype(BF16)
    r1 = x - a1.astype(F32)
    a2 = r1.astype(BF16)
    a3 = (r1 - a2.astype(F32)).astype(BF16)
    return a1, a2, a3


def _dot(a, b):
    return jnp.dot(a, b, preferred_element_type=F32)


def _ada_kernel(c_ref, w_ref, b_ref, o_ref):
    a = _silu(c_ref[...]).astype(BF16)
    o_ref[0] = _dot(a, w_ref[0].astype(BF16)) + b_ref[0]


def _ada(cc, ada_w, ada_b):
    nl = ada_w.shape[0]
    tn = 1536
    return pl.pallas_call(
        _ada_kernel,
        grid=(nl, N_MOD * D // tn),
        in_specs=[
            pl.BlockSpec((8, D), lambda l, j: (0, 0)),
            pl.BlockSpec((1, D, tn), lambda l, j: (l, 0, j)),
            pl.BlockSpec((1, 1, tn), lambda l, j: (l, 0, j)),
        ],
        out_specs=pl.BlockSpec((1, 8, tn), lambda l, j: (l, 0, j)),
        out_shape=jax.ShapeDtypeStruct((nl, 8, N_MOD * D), F32),
        compiler_params=_params(("arbitrary", "arbitrary")),
        name="ada",
    )(cc, ada_w, ada_b.reshape(nl, 1, N_MOD * D))


def _modulate(x, nw, sh, sc):
    r = lax.rsqrt(jnp.mean(x * x, axis=-1, keepdims=True) + EPS)
    return (x * r * nw) * (1.0 + sc) + sh


def _rope(y, cos, sa, sb):
    parts = []
    for j in range(D // LANES):
        yj = y[:, j * LANES:(j + 1) * LANES]
        parts.append(yj * cos + pltpu.roll(yj, LANES - 16, 1) * sa + pltpu.roll(yj, 16, 1) * sb)
    return jnp.concatenate(parts, axis=1)


def _inproj_kernel(x_ref, nw_ref, sh_ref, sc_ref, w_ref, cos_ref, sa_ref, sb_ref, k_in, v_in,
                   p_ref, k_ref, v_ref, q_ref, h_scr, *, scale):
    del k_in, v_in
    j = pl.program_id(1)
    nf = N_PROJ_F32 // D

    @pl.when(j == 0)
    def _():
        h_scr[...] = _modulate(x_ref[...], nw_ref[...], sh_ref[...], sc_ref[...]).astype(BF16)

    y = _dot(h_scr[...], w_ref[...])

    @pl.when(j < nf)
    def _():
        p_ref[...] = y

    @pl.when(j == nf)
    def _():
        k_ref[...] = _rope(y, cos_ref[...], sa_ref[...], sb_ref[...]).astype(BF16)

    @pl.when(j == nf + 1)
    def _():
        v_ref[...] = y.astype(BF16)

    @pl.when(j == nf + 2)
    def _():
        q_ref[...] = (_rope(y, cos_ref[...], sa_ref[...], sb_ref[...]) * scale).astype(BF16)


def _inproj(x, nw, sh, sc, w, tabs, scale, k_all, v_all, row_block):
    n = x.shape[0]
    tm = min(n, 1024)
    tn = D
    nf = N_PROJ_F32 // tn
    vec = pl.BlockSpec((1, D), lambda i, j: (0, 0))
    tab = pl.BlockSpec((tm, LANES), lambda i, j: (i, 0))
    kv_spec = pl.BlockSpec((tm, D), lambda i, j: (row_block + i, 0))
    kv_shape = jax.ShapeDtypeStruct(k_all.shape, BF16)
    hbm = pl.BlockSpec(memory_space=pl.ANY)
    return pl.pallas_call(
        functools.partial(_inproj_kernel, scale=scale),
        grid=(n // tm, N_PROJ // tn),
        in_specs=[pl.BlockSpec((tm, D), lambda i, j: (i, 0)), vec, vec, vec,
                  pl.BlockSpec((D, tn), lambda i, j: (0, j)), tab, tab, tab, hbm, hbm],
        out_specs=[pl.BlockSpec((tm, tn), lambda i, j: (i, jnp.minimum(j, nf - 1))), kv_spec, kv_spec,
                   pl.BlockSpec((tm, D), lambda i, j: (i, 0))],
        out_shape=[jax.ShapeDtypeStruct((n, N_PROJ_F32), F32), kv_shape, kv_shape,
                   jax.ShapeDtypeStruct((n, D), BF16)],
        scratch_shapes=[pltpu.VMEM((tm, D), BF16)],
        input_output_aliases={8: 1, 9: 2},
        compiler_params=_params(("arbitrary", "arbitrary")),
        name="inproj",
    )(x, nw, sh, sc, w, *tabs, k_all, v_all)


def _ssdconv_kernel(prev_ref, cur_ref, next_ref, w_ref, b_ref, o_ref):
    i = pl.program_id(0)
    nt = pl.num_programs(0)
    u = cur_ref[...]
    t = u.shape[0]
    row = lax.broadcasted_iota(I32, u.shape, 0)
    before = jnp.where(i > 0, prev_ref[7:8, :], 0.0)
    after = jnp.where(i < nt - 1, next_ref[0:1, :], 0.0)
    um1 = jnp.where(row == 0, before, pltpu.roll(u, 1, 0))
    up1 = jnp.where(row == t - 1, after, pltpu.roll(u, t - 1, 0))
    y = um1 * w_ref[0:1, :] + u * w_ref[1:2, :] + up1 * w_ref[2:3, :] + b_ref[...]
    o_ref[...] = _silu(y)


def _ssdconv(p, conv_w, conv_b):
    n = p.shape[0]
    t = min(n, 1024)
    cw = 512
    r8 = t // 8
    nb8 = n // 8

    def col(c):
        return jnp.where(c < 2, c, OFF_B // cw)

    return pl.pallas_call(
        _ssdconv_kernel,
        grid=(n // t, 3),
        in_specs=[
            pl.BlockSpec((8, cw), lambda i, c: (jnp.maximum(i * r8 - 1, 0), col(c))),
            pl.BlockSpec((t, cw), lambda i, c: (i, col(c))),
            pl.BlockSpec((8, cw), lambda i, c: (jnp.minimum((i + 1) * r8, nb8 - 1), col(c))),
            pl.BlockSpec((3, cw), lambda i, c: (0, c)),
            pl.BlockSpec((1, cw), lambda i, c: (0, c)),
        ],
        out_specs=pl.BlockSpec((t, cw), lambda i, c: (i, c)),
        out_shape=jax.ShapeDtypeStruct((n, 3 * cw), F32),
        compiler_params=_params(("arbitrary", "arbitrary")),
        name="ssdconv",
    )(p, p, p, conv_w, conv_b.reshape(1, -1))


def _ssd_direction(d, xs_ref, b_ref, c_ref, dt_ref, dtb, alog, st_ref, y_ref):
    L = SSD_CHUNK
    P = SSD_HEAD_DIM
    HP = SSD_HEADS * P
    GW = HP // 2

    x = dt_ref[...] + dtb
    dt = jnp.maximum(x, 0.0) + jnp.log1p(jnp.exp(-jnp.abs(x)))
    a = dt * (-jnp.exp(alog))

    li = lax.broadcasted_iota(I32, (L, L), 0)
    si = lax.broadcasted_iota(I32, (L, L), 1)
    keep = si <= li if d == 0 else si >= li
    tri = jnp.where(keep, 1.0, 0.0).astype(BF16)
    a1, a2, a3 = _split3(a)
    cum = _dot(tri, a1) + _dot(tri, a2) + _dot(tri, a3)
    cum_t = cum.T

    ex = jnp.where(lax.broadcasted_iota(I32, (LANES, HP), 0) == lax.broadcasted_iota(I32, (LANES, HP), 1) // P,
                   1.0, 0.0).astype(BF16)

    def expand(v):
        v1, v2, v3 = _split3(v)
        return _dot(v1, ex) + _dot(v2, ex) + _dot(v3, ex)

    cum_e = expand(cum)
    dt_e = expand(dt)
    tot_e = cum_e[L - 1:L, :] if d == 0 else cum_e[0:1, :]

    xdt = xs_ref[...] * dt_e
    xb = xdt.astype(BF16)
    xw = (xdt * jnp.exp(tot_e - cum_e)).astype(BF16)
    bb = b_ref[...].astype(BF16)
    cb = c_ref[...].astype(BF16)
    st = st_ref[...]
    stb = st.astype(BF16)
    grow = jnp.exp(cum_e)

    for g in range(2):
        bg = bb[:, g * SSD_STATE:(g + 1) * SSD_STATE]
        cg = cb[:, g * SSD_STATE:(g + 1) * SSD_STATE]
        scores = lax.dot_general(cg, bg, (((1,), (1,)), ((), ())), preferred_element_type=F32)
        y_off = _dot(cg, stb[:, g * GW:(g + 1) * GW]) * grow[:, g * GW:(g + 1) * GW]
        for hh in range(SSD_HEADS // 2):
            h = g * (SSD_HEADS // 2) + hh
            seg = cum[:, h:h + 1] - cum_t[h:h + 1, :]
            dec = jnp.where(keep, jnp.exp(jnp.where(keep, seg, 0.0)), 0.0)
            m = (scores * dec).astype(BF16)
            yd = _dot(m, xb[:, h * P:(h + 1) * P])
            y_ref[:, h * P:(h + 1) * P] = yd + y_off[:, hh * P:(hh + 1) * P]
        upd = lax.dot_general(bg, xw[:, g * GW:(g + 1) * GW], (((0,), (0,)), ((), ())),
                              preferred_element_type=F32)
        st_ref[:, g * GW:(g + 1) * GW] = st[:, g * GW:(g + 1) * GW] * jnp.exp(tot_e[:, g * GW:(g + 1) * GW]) + upd


def _ssd_kernel(xsf_ref, bf_ref, cf_ref, dtf_ref, xsb_ref, bb_ref, cb_ref, dtb_ref, bias_ref, alog_ref, h0_ref,
                yf_ref, yb_ref, hT_ref, st_scr):
    c = pl.program_id(0)

    @pl.when(c == 0)
    def _():
        st_scr[...] = h0_ref[...]

    L = SSD_CHUNK
    nsub = xsf_ref.shape[0] // L
    for k in range(nsub):
        f = slice(k * L, (k + 1) * L)
        r = slice((nsub - 1 - k) * L, (nsub - k) * L)
        _ssd_direction(0, xsf_ref.at[f], bf_ref.at[f], cf_ref.at[f], dtf_ref.at[f], bias_ref[0], alog_ref[0],
                       st_scr.at[0], yf_ref.at[f])
        _ssd_direction(1, xsb_ref.at[r], bb_ref.at[r], cb_ref.at[r], dtb_ref.at[r], bias_ref[1], alog_ref[1],
                       st_scr.at[1], yb_ref.at[r])

    @pl.when(c == pl.num_programs(0) - 1)
    def _():
        hT_ref[...] = st_scr[...]


def _ssd(xbc, p, dtb, alog, h0):
    n = xbc.shape[0]
    L = SSD_CHUNK * (2 if n % (2 * SSD_CHUNK) == 0 else 1)
    nc = n // L
    HP = SSD_HEADS * SSD_HEAD_DIM
    full = pl.BlockSpec((2, SSD_STATE, HP), lambda c: (0, 0, 0))
    vec = pl.BlockSpec((2, 1, LANES), lambda c: (0, 0, 0))

    def side(chunk, d):
        return [
            pl.BlockSpec((L, HP), lambda c: (chunk(c), 0)),
            pl.BlockSpec((L, SSD_GN), lambda c: (chunk(c), HP // SSD_GN)),
            pl.BlockSpec((L, SSD_GN), lambda c: (chunk(c), HP // SSD_GN + 1)),
            pl.BlockSpec((L, LANES), lambda c: (chunk(c), OFF_DT // LANES + d)),
        ]

    def fwd(c):
        return c

    def bwd(c):
        return nc - 1 - c

    return pl.pallas_call(
        _ssd_kernel,
        grid=(nc,),
        in_specs=side(fwd, 0) + side(bwd, 1) + [vec, vec, full],
        out_specs=[pl.BlockSpec((L, HP), lambda c: (c, 0)), pl.BlockSpec((L, HP), lambda c: (nc - 1 - c, 0)), full],
        out_shape=[jax.ShapeDtypeStruct((n, HP), F32), jax.ShapeDtypeStruct((n, HP), F32),
                   jax.ShapeDtypeStruct((2, SSD_STATE, HP), F32)],
        scratch_shapes=[pltpu.VMEM((2, SSD_STATE, HP), F32)],
        compiler_params=_params(("arbitrary",)),
        name="ssd",
    )(xbc, xbc, xbc, p, xbc, xbc, xbc, p, dtb, alog, h0)


def _attn_kernel(lam_ref, q_ref, k_ref, v_ref, sw_ref, o_ref, q2_scr, sa_scr, sb_scr, p_scr, m_scr, acc_scr,
                 *, tq, nq, ck, nchunks, coef):
    lane = lax.broadcasted_iota(I32, (tq, LANES), 1)
    for t in range(nq):
        q = q_ref[t * tq:(t + 1) * tq, :]
        zero = jnp.zeros_like(q)
        q2_scr[t, 0:tq, :] = jnp.where(lane < DIFF_HEAD_DIM, q, zero)
        q2_scr[t, tq:2 * tq, :] = jnp.where(lane >= DIFF_HEAD_DIM, q, zero)
    m_scr[...] = jnp.full(m_scr.shape, -jnp.inf, F32)
    acc_scr[...] = jnp.zeros(acc_scr.shape, F32)
    ones = jnp.ones((ck, LANES), BF16)
    nb = ck // LANES

    def key_rows(c):
        return pl.ds(c * ck, ck) if isinstance(c, int) else pl.ds(pl.multiple_of(c * ck, ck), ck)

    def scores(t, c, s_scr):
        s_scr[...] = lax.dot_general(q2_scr[t], k_ref[key_rows(c), :], (((1,), (1,)), ((), ())),
                                     preferred_element_type=F32)

    def soft_pv(t, c, s_scr):
        rows = key_rows(c)
        mx = s_scr[:, 0:LANES]
        for b in range(1, nb):
            mx = jnp.maximum(mx, s_scr[:, b * LANES:(b + 1) * LANES])
        m_prev = m_scr[t]
        m_new = jnp.maximum(m_prev, jnp.max(mx, axis=1, keepdims=True))
        alpha = jnp.exp2(m_prev - m_new)
        for b in range(nb):
            p_scr[:, b * LANES:(b + 1) * LANES] = jnp.exp2(s_scr[:, b * LANES:(b + 1) * LANES] - m_new).astype(BF16)
        va = jnp.concatenate([v_ref[rows, :], ones], axis=1)
        acc_scr[t] = acc_scr[t] * jnp.concatenate([alpha, alpha], axis=1) + _dot(p_scr[...], va)
        m_scr[t] = m_new

    def tile(t, cur, nxt):
        def pair(jj, carry):
            scores(t, 2 * jj + 1, nxt)
            soft_pv(t, 2 * jj, cur)
            scores(t, 2 * jj + 2, cur)
            soft_pv(t, 2 * jj + 1, nxt)
            return carry

        npairs = (nchunks - 1) // 2
        lax.fori_loop(0, npairs, pair, 0, unroll=2 if npairs % 2 == 0 else 1)
        last, free = cur, nxt
        if (nchunks - 1) % 2 == 1:
            scores(t, nchunks - 1, nxt)
            soft_pv(t, nchunks - 2, cur)
            last, free = nxt, cur
        if t + 1 < nq:
            scores(t + 1, 0, free)
        soft_pv(t, nchunks - 1, last)
        return free, last

    scores(0, 0, sa_scr)
    bufs = (sa_scr, sb_scr)
    for t in range(nq):
        bufs = tile(t, *bufs)

    for t in range(nq):
        on = acc_scr[t, :, 0:LANES] / acc_scr[t, :, LANES:2 * LANES]
        o = on[0:tq, :] - lam_ref[0] * on[tq:2 * tq, :]
        r = lax.rsqrt(jnp.mean(o * o, axis=-1, keepdims=True) + EPS)
        o_ref[t * tq:(t + 1) * tq, :] = ((o * r * sw_ref[...]) * coef).astype(o_ref.dtype)


def _pick_tk(m):
    for cand in (1280, 1024, 512, 256):
        if m % cand == 0:
            return cand
    return m


def _attention(lam, q, k, v, subln_w, m, key_block, coef):
    n = q.shape[0]
    tq = min(n, 512)
    nq = min(2, n // tq)
    ck = _pick_tk(m)
    return pl.pallas_call(
        functools.partial(_attn_kernel, tq=tq, nq=nq, ck=ck, nchunks=m // ck, coef=coef),
        grid=(DIFF_HEADS, n // (nq * tq)),
        in_specs=[
            pl.BlockSpec(memory_space=pltpu.SMEM),
            pl.BlockSpec((nq * tq, LANES), lambda h, i: (i, h)),
            pl.BlockSpec((m, LANES), lambda h, i: (key_block, h)),
            pl.BlockSpec((m, LANES), lambda h, i: (key_block, h)),
            pl.BlockSpec((1, LANES), lambda h, i: (0, 0)),
        ],
        out_specs=pl.BlockSpec((nq * tq, LANES), lambda h, i: (i, h)),
        out_shape=jax.ShapeDtypeStruct((n, D), BF16),
        scratch_shapes=[pltpu.VMEM((nq, 2 * tq, LANES), BF16), pltpu.VMEM((2 * tq, ck), F32),
                        pltpu.VMEM((2 * tq, ck), F32), pltpu.VMEM((2 * tq, ck), BF16),
                        pltpu.VMEM((nq, 2 * tq, LANES), F32), pltpu.VMEM((nq, 2 * tq, 2 * LANES), F32)],
        compiler_params=_params(("arbitrary", "arbitrary")),
        name="attn",
    )(lam, q, k, v, subln_w)


def _conf_kernel(ap_ref, gp_ref, a_ref, g_ref, an_ref, gn_ref, w_ref, b_ref, lw_ref, lb_ref, o_ref, ext_scr, sh_scr):
    i = pl.program_id(0)
    nt = pl.num_programs(0)
    t = a_ref.shape[0]
    hl = CONF_HALO

    def glu(a, g):
        return a[...] * jax.nn.sigmoid(g[...])

    ext_scr[0:hl, :] = jnp.where(i > 0, glu(ap_ref, gp_ref), 0.0)
    ext_scr[hl:hl + t, :] = glu(a_ref, g_ref)
    ext_scr[hl + t:2 * hl + t, :] = jnp.where(i < nt - 1, glu(an_ref, gn_ref), 0.0)

    span = t + 2 * hl - 8
    for r in range(1, 8):
        sh_scr[r, 0:span, :] = ext_scr[pl.ds(r, span), :]
    acc = jnp.zeros((t, D), F32) + b_ref[...]
    for k in range(CONF_KERNEL):
        off = k + hl - CONF_KERNEL // 2
        src = ext_scr if off % 8 == 0 else sh_scr.at[off % 8]
        acc = acc + src[pl.ds(off - off % 8, t), :] * w_ref[k:k + 1, :]
    mu = jnp.mean(acc, axis=-1, keepdims=True)
    xc = acc - mu
    y = xc * lax.rsqrt(jnp.mean(xc * xc, axis=-1, keepdims=True) + EPS) * lw_ref[...] + lb_ref[...]
    o_ref[...] = _silu(y).astype(o_ref.dtype)


def _conformer(p, dw_w, dw_b, ln_w, ln_b):
    n = p.shape[0]
    t = min(n, 512)
    hl = CONF_HALO
    rh = t // hl
    nbh = n // hl
    ca, cg = OFF_GLU_A // D, OFF_GLU_G // D

    def prev(i):
        return jnp.maximum(i * rh - 1, 0)

    def nxt(i):
        return jnp.minimum((i + 1) * rh, nbh - 1)

    vec = pl.BlockSpec((1, D), lambda i: (0, 0))
    return pl.pallas_call(
        _conf_kernel,
        grid=(n // t,),
        in_specs=[
            pl.BlockSpec((hl, D), lambda i: (prev(i), ca)), pl.BlockSpec((hl, D), lambda i: (prev(i), cg)),
            pl.BlockSpec((t, D), lambda i: (i, ca)), pl.BlockSpec((t, D), lambda i: (i, cg)),
            pl.BlockSpec((hl, D), lambda i: (nxt(i), ca)), pl.BlockSpec((hl, D), lambda i: (nxt(i), cg)),
            pl.BlockSpec((CONF_KERNEL, D), lambda i: (0, 0)), vec, vec, vec,
        ],
        out_specs=pl.BlockSpec((t, D), lambda i: (i, 0)),
        out_shape=jax.ShapeDtypeStruct((n, D), BF16),
        scratch_shapes=[pltpu.VMEM((t + 2 * hl, D), F32), pltpu.VMEM((8, t + 2 * hl, D), F32)],
        compiler_params=_params(("arbitrary",)),
        name="conformer",
    )(p, p, p, p, p, p, dw_w, dw_b.reshape(1, D), ln_w.reshape(1, D), ln_b.reshape(1, D))


def _merge_kernel(yf_ref, yb_ref, xs_ref, z_ref, on_ref, cn_ref, g0_ref, g1_ref, g2_ref, x_ref, dsk_ref, nw_ref,
                  m2_ref, wa_ref, wb_ref, wc_ref, wo_ref, o_ref):
    y = yf_ref[...] + yb_ref[...] + xs_ref[...] * dsk_ref[...]
    yz = y * _silu(z_ref[...])
    half = D // 2
    parts = []
    for g in range(2):
        seg = yz[:, g * half:(g + 1) * half]
        r = lax.rsqrt(jnp.mean(seg * seg, axis=-1, keepdims=True) + EPS)
        parts.append(seg * r * nw_ref[:, g * half:(g + 1) * half])
    gn = jnp.concatenate(parts, axis=1).astype(BF16)
    y_a = _dot(gn, wa_ref[...])
    y_b = _dot(on_ref[...], wb_ref[...])
    y_c = _dot(cn_ref[...], wc_ref[...])
    m = jax.nn.sigmoid(g0_ref[...]) * y_a + jax.nn.sigmoid(g1_ref[...]) * y_b + jax.nn.sigmoid(g2_ref[...]) * y_c
    y_o = _dot(m.astype(BF16), wo_ref[...])
    o_ref[...] = x_ref[...] + m2_ref[...] * y_o


def _merge(yf, yb, xbc, p, on, cn, x, dsk, nw, m2, wa, wb, wc, wo):
    n = x.shape[0]
    tm = min(n, 256)
    vec = pl.BlockSpec((1, D), lambda i: (0, 0))
    wsp = pl.BlockSpec((D, D), lambda i: (0, 0))
    row = pl.BlockSpec((tm, D), lambda i: (i, 0))
    return pl.pallas_call(
        _merge_kernel,
        grid=(n // tm,),
        in_specs=[
            row, row,
            row,
            pl.BlockSpec((tm, D), lambda i: (i, OFF_Z // D)),
            row, row,
            pl.BlockSpec((tm, D), lambda i: (i, OFF_GATE // D)),
            pl.BlockSpec((tm, D), lambda i: (i, OFF_GATE // D + 1)),
            pl.BlockSpec((tm, D), lambda i: (i, OFF_GATE // D + 2)),
            row, vec, vec, vec, wsp, wsp, wsp, wsp,
        ],
        out_specs=row,
        out_shape=jax.ShapeDtypeStruct((n, D), F32),
        compiler_params=_params(("arbitrary",)),
        name="merge",
    )(yf, yb, xbc, p, on, cn, p, p, p, x, dsk, nw, m2, wa, wb, wc, wo)


def _router_kernel(x_ref, nw_ref, sh_ref, sc_ref, rw_ref, h_ref, aff_ref, afft_ref):
    h = _modulate(x_ref[...], nw_ref[...], sh_ref[...], sc_ref[...]).astype(BF16)
    h_ref[...] = h
    logits = _dot(h, rw_ref[...])
    lane = lax.broadcasted_iota(I32, logits.shape, 1)
    logits = jnp.where(lane < N_EXPERTS, logits, -jnp.inf)
    e = jnp.exp(logits - jnp.max(logits, axis=-1, keepdims=True))
    aff = e / jnp.sum(e, axis=-1, keepdims=True)
    aff_ref[...] = aff
    afft_ref[...] = aff.T[0:N_EXPERTS, :]


def _router(x, nw, sh, sc, rw):
    n = x.shape[0]
    tm = min(n, 512)
    vec = pl.BlockSpec((1, D), lambda i: (0, 0))
    return pl.pallas_call(
        _router_kernel,
        grid=(n // tm,),
        in_specs=[pl.BlockSpec((tm, D), lambda i: (i, 0)), vec, vec, vec, pl.BlockSpec((D, LANES), lambda i: (0, 0))],
        out_specs=[pl.BlockSpec((tm, D), lambda i: (i, 0)), pl.BlockSpec((tm, LANES), lambda i: (i, 0)),
                   pl.BlockSpec((N_EXPERTS, tm), lambda i: (0, i))],
        out_shape=[jax.ShapeDtypeStruct((n, D), BF16), jax.ShapeDtypeStruct((n, LANES), F32),
                   jax.ShapeDtypeStruct((N_EXPERTS, n), F32)],
        compiler_params=_params(("arbitrary",)),
        name="router",
    )(x, nw, sh, sc, rw)


def _select_kernel(a_ref, pose_ref, post_ref, off_ref, *, cap):
    n = a_ref.shape[1]
    nb = n // LANES
    def search(it, thr):
        bits = pltpu.bitcast(a_ref[...], I32)
        cand = thr | lax.shift_left(jnp.int32(1), 30 - it)
        cnt = jnp.sum(jnp.where(bits >= cand, 1.0, 0.0), axis=1, keepdims=True)
        return jnp.where(cnt >= cap, cand, thr)

    thr = lax.fori_loop(0, 31, search, jnp.zeros((N_EXPERTS, 1), I32))
    n_gt = jnp.sum(jnp.where(pltpu.bitcast(a_ref[...], I32) > thr, 1.0, 0.0), axis=1, keepdims=True)
    need = cap - n_gt

    tri = jnp.where(lax.broadcasted_iota(I32, (LANES, LANES), 0) <= lax.broadcasted_iota(I32, (LANES, LANES), 1),
                    1.0, 0.0).astype(BF16)

    def block(b, carry):
        c_eq, c_sel = carry
        start = pl.multiple_of(b * LANES, LANES)
        bb = pltpu.bitcast(a_ref[:, pl.ds(start, LANES)], I32)
        gt = jnp.where(bb > thr, 1.0, 0.0)
        eq = jnp.where(bb == thr, 1.0, 0.0)
        inc_eq = _dot(eq.astype(BF16), tri)
        rank_eq = c_eq + inc_eq - eq
        sel = gt + eq * jnp.where(rank_eq < need, 1.0, 0.0)
        inc_sel = _dot(sel.astype(BF16), tri)
        pos = jnp.where(sel > 0.0, c_sel + inc_sel - sel, -1.0)
        pose_ref[:, pl.ds(start, LANES)] = pos.astype(I32)
        full = jnp.concatenate([pos, jnp.full((LANES - N_EXPERTS, LANES), -1.0, F32)], axis=0)
        post_ref[pl.ds(start, LANES), :] = full.T
        off_ref[b] = jnp.broadcast_to(c_sel, (N_EXPERTS, LANES))
        return c_eq + inc_eq[:, LANES - 1:LANES], c_sel + inc_sel[:, LANES - 1:LANES]

    zero = jnp.zeros((N_EXPERTS, 1), F32)
    lax.fori_loop(0, nb, block, (zero, zero))


def _select(afft, cap):
    n = afft.shape[1]
    nb = n // LANES
    return pl.pallas_call(
        functools.partial(_select_kernel, cap=cap),
        out_shape=[jax.ShapeDtypeStruct((N_EXPERTS, n), I32), jax.ShapeDtypeStruct((n, LANES), F32),
                   jax.ShapeDtypeStruct((nb, N_EXPERTS, LANES), F32)],
        compiler_params=_params(None),
        name="select",
    )(afft)


def _gather_kernel(jt_ref, tt_ref, fl_ref, pos_ref, h_ref, o_ref, acc_scr, *, sb, kp):
    e = pl.program_id(0)
    k = pl.program_id(1)
    idx = e * kp + k
    j = jt_ref[idx]
    flags = fl_ref[idx]

    @pl.when((flags & 1) != 0)
    def _():
        tb = pos_ref.shape[2]
        kb = min(tb, 256)
        slot = lax.broadcasted_iota(I32, (sb, kb), 0) + j * sb
        rows = jnp.zeros((sb, D), F32)
        for t in range(tb // kb):
            onehot = jnp.where(pos_ref[0, :, t * kb:(t + 1) * kb] == slot, 1.0, 0.0).astype(BF16)
            rows = rows + _dot(onehot, h_ref[t * kb:(t + 1) * kb, :])

        @pl.when((flags & 2) != 0)
        def _():
            acc_scr[...] = rows

        @pl.when((flags & 2) == 0)
        def _():
            acc_scr[...] = acc_scr[...] + rows

    @pl.when((flags & 4) != 0)
    def _():
        o_ref[0] = acc_scr[...].astype(o_ref.dtype)


def _gather(jt, tt, fl, pose, h2, cap, sb, tb, kp):
    n = h2.shape[0]
    nt = n // tb
    pos3 = pose.reshape(N_EXPERTS * nt, 1, tb)
    grid_spec = pltpu.PrefetchScalarGridSpec(
        num_scalar_prefetch=3,
        grid=(N_EXPERTS, kp),
        in_specs=[
            pl.BlockSpec((1, 1, tb), lambda e, k, jt, tt, fl: (e * nt + tt[e * kp + k], 0, 0)),
            pl.BlockSpec((tb, D), lambda e, k, jt, tt, fl: (tt[e * kp + k], 0)),
        ],
        out_specs=pl.BlockSpec((1, sb, D), lambda e, k, jt, tt, fl: (e, jt[e * kp + k], 0)),
        scratch_shapes=[pltpu.VMEM((sb, D), F32)],
    )
    return pl.pallas_call(
        functools.partial(_gather_kernel, sb=sb, kp=kp),
        grid_spec=grid_spec,
        out_shape=jax.ShapeDtypeStruct((N_EXPERTS, cap, D), BF16),
        compiler_params=_params(("arbitrary", "arbitrary")),
        name="gather",
    )(jt, tt, fl, pos3, h2)


def _ffn_kernel(*refs, nsets):
    x_refs = refs[:nsets]
    w1_ref, w3_ref, w2_ref = refs[nsets:nsets + 3]
    o_refs = refs[nsets + 3:2 * nsets + 3]
    acc_refs = refs[2 * nsets + 3:]
    f = pl.program_id(1)
    nf = pl.num_programs(1)

    @pl.when(f == 0)
    def _():
        for acc in acc_refs:
            acc[...] = jnp.zeros(acc.shape, F32)

    w1 = w1_ref[0, 0].astype(BF16)
    w3 = w3_ref[0, 0].astype(BF16)
    w2 = w2_ref[0, 0].astype(BF16)
    for x_ref, acc in zip(x_refs, acc_refs):
        cap = acc.shape[0]
        rb = min(cap, 512)
        for r in range(cap // rb):
            x = x_ref[0, r * rb:(r + 1) * rb, :]
            he = (_silu(_dot(x, w1)) * _dot(x, w3)).astype(BF16)
            acc[r * rb:(r + 1) * rb, :] = acc[r * rb:(r + 1) * rb, :] + _dot(he, w2)

    @pl.when(f == nf - 1)
    def _():
        for o_ref, acc in zip(o_refs, acc_refs):
            o_ref[0] = acc[...].astype(o_ref.dtype)


def _ffn(xes, w1, w3, w2, l):
    fc = 256

    def rows(xe):
        return pl.BlockSpec((1, xe.shape[1], D), lambda e, f: (e, 0, 0))

    return pl.pallas_call(
        functools.partial(_ffn_kernel, nsets=len(xes)),
        grid=(N_EXPERTS, EXPERT_FF // fc),
        in_specs=[rows(xe) for xe in xes] + [
            pl.BlockSpec((1, 1, D, fc), lambda e, f: (l, e, 0, f)),
            pl.BlockSpec((1, 1, D, fc), lambda e, f: (l, e, 0, f)),
            pl.BlockSpec((1, 1, fc, D), lambda e, f: (l, e, f, 0)),
        ],
        out_specs=[rows(xe) for xe in xes],
        out_shape=[jax.ShapeDtypeStruct(xe.shape, BF16) for xe in xes],
        scratch_shapes=[pltpu.VMEM((xe.shape[1], D), F32) for xe in xes],
        compiler_params=_params(("arbitrary", "arbitrary")),
        name="ffn",
    )(*xes, w1, w3, w2)


def _combine_kernel(w0_ref, cnt_ref, pos_ref, gate_ref, x_ref, m5_ref, fw_ref, ye_hbm, o_ref, ybuf, xbuf, acc_scr, sem,
                    *, cap, win, final):
    b = pl.program_id(0)
    nblk = pl.num_programs(0)
    tb = pos_ref.shape[0]
    slot = lax.broadcasted_iota(I32, (tb, win), 1).astype(F32)
    half = b % 2

    def window(blk, e, wb):
        lo = (w0_ref[blk * N_EXPERTS + e] // BF16_ROWS) * BF16_ROWS + wb * win
        start = pl.multiple_of(jnp.minimum(lo, cap - win), BF16_ROWS)
        return lo, start

    def fetch(e, start, buf, s):
        return pltpu.make_async_copy(ye_hbm.at[e, pl.ds(start, win), :], buf, s)

    def first_window(blk, h, e):
        return fetch(e, window(blk, e, 0)[1], ybuf.at[h, e], sem.at[h * N_EXPERTS + e])

    @pl.when(b == 0)
    def _():
        for e in range(N_EXPERTS):
            first_window(0, 0, e).start()

    @pl.when(b + 1 < nblk)
    def _():
        for e in range(N_EXPERTS):
            first_window(b + 1, 1 - half, e).start()

    for e in range(N_EXPERTS):
        first_window(b, half, e).wait()
    acc = jnp.zeros((tb, D), F32)
    for e in range(N_EXPERTS):
        start = window(b, e, 0)[1]
        onehot = jnp.where(pos_ref[:, e:e + 1] - start.astype(F32) == slot, 1.0, 0.0).astype(BF16)
        acc = acc + _dot(onehot, ybuf[half, e]) * gate_ref[:, e:e + 1]
    acc_scr[...] = acc

    for e in range(N_EXPERTS):
        w0 = w0_ref[b * N_EXPERTS + e]
        n_win = (w0 % BF16_ROWS + cnt_ref[b * N_EXPERTS + e] + win - 1) // win

        def extra(wb, carry, e=e):
            lo, start = window(b, e, wb)
            cp = fetch(e, start, xbuf, sem.at[2 * N_EXPERTS])
            cp.start()
            cp.wait()
            posc = pos_ref[:, e:e + 1]
            hit = jnp.where(posc - start.astype(F32) == slot, 1.0, 0.0) * jnp.where(posc >= lo.astype(F32), 1.0, 0.0)
            acc_scr[...] = acc_scr[...] + _dot(hit.astype(BF16), xbuf[...]) * gate_ref[:, e:e + 1]
            return carry

        lax.fori_loop(1, n_win, extra, 0)

    y = x_ref[...] + m5_ref[...] * acc_scr[...]
    if final:
        y = y * lax.rsqrt(jnp.mean(y * y, axis=-1, keepdims=True) + EPS) * fw_ref[...]
    o_ref[...] = y


def _combine(w0, cnt, post, aff, ye, x, m5, fw, tb, final):
    n = x.shape[0]
    cap = ye.shape[1]
    win = min(COMBINE_WIN, cap)
    vec = pl.BlockSpec((1, D), lambda b, w0, cnt: (0, 0))
    grid_spec = pltpu.PrefetchScalarGridSpec(
        num_scalar_prefetch=2,
        grid=(n // tb,),
        in_specs=[
            pl.BlockSpec((tb, LANES), lambda b, w0, cnt: (b, 0)),
            pl.BlockSpec((tb, LANES), lambda b, w0, cnt: (b, 0)),
            pl.BlockSpec((tb, D), lambda b, w0, cnt: (b, 0)),
            vec, vec,
            pl.BlockSpec(memory_space=pl.ANY),
        ],
        out_specs=pl.BlockSpec((tb, D), lambda b, w0, cnt: (b, 0)),
        scratch_shapes=[pltpu.VMEM((2, N_EXPERTS, win, D), BF16), pltpu.VMEM((win, D), BF16),
                        pltpu.VMEM((tb, D), F32), pltpu.SemaphoreType.DMA((2 * N_EXPERTS + 1,))],
    )
    return pl.pallas_call(
        functools.partial(_combine_kernel, cap=cap, win=win, final=final),
        grid_spec=grid_spec,
        out_shape=jax.ShapeDtypeStruct((n, D), F32),
        compiler_params=_params(("arbitrary",)),
        name="combine",
    )(w0, cnt, post, aff, x, m5, fw, ye)


def _pair_tables(off_tb, cap, sb, kp):
    ne, nt = off_tb.shape
    nxt = jnp.concatenate([off_tb[:, 1:], jnp.full((ne, 1), cap, I32)], axis=1)
    cnt = nxt - off_tb
    j_lo = off_tb // sb
    j_hi = jnp.where(cnt > 0, (nxt - 1) // sb, j_lo - 1)
    npair = j_hi - j_lo + 1
    starts = jnp.cumsum(npair, axis=1) - npair
    total = jnp.sum(npair, axis=1, keepdims=True)
    k = jnp.arange(kp, dtype=I32)[None, :]
    kk = jnp.minimum(k, total - 1)
    tb_k = jnp.sum((starts[:, None, :] <= kk[:, :, None]).astype(I32), axis=2) - 1
    j_k = jnp.take_along_axis(j_lo, tb_k, axis=1) + kk - jnp.take_along_axis(starts, tb_k, axis=1)
    valid = (k < total).astype(I32)
    prev_j = jnp.concatenate([jnp.full((ne, 1), -1, I32), j_k[:, :-1]], axis=1)
    first = ((j_k != prev_j) & (k < total)).astype(I32)
    next_j = jnp.concatenate([j_k[:, 1:], jnp.full((ne, 1), -1, I32)], axis=1)
    last = (((j_k != next_j) | (k + 1 >= total)) & (k < total)).astype(I32)
    flags = valid + 2 * first + 4 * last
    return j_k.reshape(-1).astype(I32), tb_k.reshape(-1).astype(I32), flags.reshape(-1).astype(I32)


def _moe_route(x, nw, sh, sc, rw):
    n = x.shape[0]
    cap = CAPACITY_FACTOR * n // N_EXPERTS
    sb = min(256, cap)
    tbg = min(1024, n)
    kp = cap // sb + n // tbg
    h2, aff, afft = _router(x, nw, sh, sc, rw)
    pose, post, off = _select(afft, cap)
    off128 = off[:, :, 0].T.astype(I32)
    jt, tt, fl = _pair_tables(off128[:, :: tbg // LANES], cap, sb, kp)
    xe = _gather(jt, tt, fl, pose, h2, cap, sb, tbg, kp)
    return xe, (post, aff, off128)


def _moe_finish(x, ye, route, m5, fw, final):
    post, aff, off128 = route
    n = x.shape[0]
    cap = ye.shape[1]
    tbc = min(256, n)
    off_c = off128[:, :: tbc // LANES]
    cnt_c = jnp.concatenate([off_c[:, 1:], jnp.full((N_EXPERTS, 1), cap, I32)], axis=1) - off_c
    return _combine(off_c.T.reshape(-1), cnt_c.T.reshape(-1), post, aff, ye, x, m5, fw, tbc, final)


def _rope_tables(n_rows):
    row = jnp.repeat(jnp.arange(n_rows, dtype=F32), GRID_W)
    col = jnp.tile(jnp.arange(GRID_W, dtype=F32), n_rows)
    half = DIFF_HEAD_DIM // 2
    freqs = ROPE_BASE ** (-jnp.arange(0, half, 2, dtype=F32) / half)
    ar, ac = row[:, None] * freqs, col[:, None] * freqs
    cr, sr, cc, sn = jnp.cos(ar), jnp.sin(ar), jnp.cos(ac), jnp.sin(ac)
    z = jnp.zeros_like(sr)
    cos = jnp.concatenate([cr, cr, cc, cc], axis=1)
    sa = jnp.concatenate([-sr, z, -sn, z], axis=1)
    sb = jnp.concatenate([z, sr, z, sn], axis=1)
    return tuple(jnp.tile(t, (1, 2)) for t in (cos, sa, sb))


def _identity_tables(n):
    return (jnp.ones((n, LANES), F32), jnp.zeros((n, LANES), F32), jnp.zeros((n, LANES), F32))


def _proj_weight(w_in):
    sizes = (1024, 256, 32, 1024, 1024, 256, 1024, 1024, 2048, 3072)
    idx = np.cumsum((0,) + sizes)
    px, pb, pdt, pk, pv, pc, pq, pz, pglu, pgate = [w_in[:, idx[i]:idx[i + 1]] for i in range(10)]
    pad = jnp.zeros((D, LANES - SSD_HEADS), w_in.dtype)
    tail = jnp.zeros((D, N_PROJ_F32 - OFF_DT - 2 * LANES), w_in.dtype)
    w = jnp.concatenate([px, pz, pglu, pgate, pb, pc, pdt[:, :SSD_HEADS], pad, pdt[:, SSD_HEADS:], pad, tail,
                         pk, pv, pq], axis=1)
    return w.astype(BF16)


def _lane_pad(v):
    return jnp.pad(v, ((0, 0), (0, LANES - v.shape[1]))).reshape(2, 1, LANES)


def kernel(x, c, ctx, c_ctx, ada_w, ada_b, norm1_w, norm2_w, w_in, ssd_conv_w, ssd_conv_b, ssd_dt_bias, ssd_a_log, ssd_d, ssd_norm_w, ssd_out, diff_lambda, diff_subln_w, diff_out, conf_dw_w, conf_dw_b, conf_ln_w, conf_ln_b, conf_out, w_o, router_w, exp_w1, exp_w3, exp_w2, final_norm_w):
    depth = ada_w.shape[0]
    n = x.shape[1]
    nctx = ctx.shape[1]
    x_lat = x[0]
    x_ctx = ctx[0]

    cc = jnp.zeros((8, D), F32).at[0].set(c[0]).at[1].set(c_ctx)
    mods = _ada(cc, ada_w, ada_b)

    rope = _rope_tables(n // GRID_W)
    ident_ctx = _identity_tables(nctx)
    scale = DIFF_HEAD_DIM ** -0.5 * math.log2(math.e)
    zero_state = jnp.zeros((2, SSD_STATE, SSD_HEADS * SSD_HEAD_DIM), F32)
    fw = final_norm_w.reshape(1, D)

    for l in range(depth):
        last = l == depth - 1
        lam_init = 0.8 - 0.6 * math.exp(-0.3 * l)
        lq1, lk1, lq2, lk2 = diff_lambda[l].astype(F32)
        lam = (jnp.exp(jnp.sum(lq1 * lk1)) - jnp.exp(jnp.sum(lq2 * lk2)) + lam_init).reshape(1)
        m_lat = [mods[l, 0:1, i * D:(i + 1) * D] for i in range(N_MOD)]
        m_ctx = [mods[l, 1:2, i * D:(i + 1) * D] for i in range(N_MOD)]
        nw1 = norm1_w[l].reshape(1, D)
        nw2 = norm2_w[l].reshape(1, D)
        w = _proj_weight(w_in[l])
        dtb = _lane_pad(ssd_dt_bias[l])
        alog = _lane_pad(ssd_a_log[l])
        dsk = jnp.repeat(ssd_d[l], SSD_HEAD_DIM).reshape(1, D)
        gnw = ssd_norm_w[l].reshape(1, D)
        sw = diff_subln_w[l].reshape(1, LANES)
        wa, wb, wc, wo = (t[l].astype(BF16) for t in (ssd_out, diff_out, conf_out, w_o))
        rw = jnp.pad(router_w[l], ((0, 0), (0, LANES - N_EXPERTS))).astype(BF16)
        coef = 1.0 - lam_init

        k_all = jnp.zeros((n + nctx, D), BF16)
        v_all = jnp.zeros((n + nctx, D), BF16)
        p_l, k_all, v_all, q_l = _inproj(x_lat, nw1, m_lat[0], m_lat[1], w, rope, scale, k_all, v_all, 0)
        p_c, k_all, v_all, q_c = _inproj(x_ctx, nw1, m_ctx[0], m_ctx[1], w, ident_ctx, scale, k_all, v_all, n // nctx)
        xbc_c = _ssdconv(p_c, ssd_conv_w[l], ssd_conv_b[l])
        xbc_l = _ssdconv(p_l, ssd_conv_w[l], ssd_conv_b[l])
        yf_c, yb_c, st_c = _ssd(xbc_c, p_c, dtb, alog, zero_state)

        if not last:
            on_c = _attention(lam, q_c, k_all, v_all, sw, nctx, n // nctx, coef)
            cn_c = _conformer(p_c, conf_dw_w[l], conf_dw_b[l], conf_ln_w[l], conf_ln_b[l])
            x_ctx = _merge(yf_c, yb_c, xbc_c, p_c, on_c, cn_c, x_ctx, dsk, gnw, m_ctx[2], wa, wb, wc, wo)
            xe_c, route_c = _moe_route(x_ctx, nw2, m_ctx[3], m_ctx[4], rw)

        yf_l, yb_l, _ = _ssd(xbc_l, p_l, dtb, alog, st_c)
        on_l = _attention(lam, q_l, k_all, v_all, sw, nctx + n, 0, coef)
        cn_l = _conformer(p_l, conf_dw_w[l], conf_dw_b[l], conf_ln_w[l], conf_ln_b[l])
        x_lat = _merge(yf_l, yb_l, xbc_l, p_l, on_l, cn_l, x_lat, dsk, gnw, m_lat[2], wa, wb, wc, wo)
        xe_l, route_l = _moe_route(x_lat, nw2, m_lat[3], m_lat[4], rw)

        if last:
            (ye_l,) = _ffn([xe_l], exp_w1, exp_w3, exp_w2, l)
        else:
            ye_l, ye_c = _ffn([xe_l, xe_c], exp_w1, exp_w3, exp_w2, l)
            x_ctx = _moe_finish(x_ctx, ye_c, route_c, m_ctx[5], fw, False)
        x_lat = _moe_finish(x_lat, ye_l, route_l, m_lat[5], fw, last)

    return x_lat[None]
```
